```python
import math
import jax, jax.numpy as jnp
from jax import lax
import numpy as np

D_MODEL = 1024
BATCH = 8
SEQ = 4096
DEPTH = 4

MIX_WIDTH = D_MODEL
FOX_WIDTH = MIX_WIDTH // 2
FOX_HEAD_DIM = 64
FOX_HEADS = FOX_WIDTH // FOX_HEAD_DIM
POOL_WIDTH = MIX_WIDTH - FOX_WIDTH
POOL_WINDOWS = (2, 4, 8, 16)
POOL_GROUPS = len(POOL_WINDOWS)
POOL_GROUP_DIM = POOL_WIDTH // POOL_GROUPS
Q_BLOCK = 128
MEM_LEN = 256
X_HEADS = 4
X_HEAD_DIM = D_MODEL // X_HEADS
D_FF = 4 * D_MODEL
EPS = 1e-6
IN_COLS = 3 * FOX_WIDTH + FOX_HEADS + POOL_WIDTH

kernel_name = "fox_pool_hybrid_memory_trunk"


def rms_norm(x, g):
    x32 = x.astype(jnp.float32)
    y = x32 * lax.rsqrt(jnp.mean(x32 * x32, axis=-1, keepdims=True) + EPS)
    return (y * g.astype(jnp.float32)).astype(x.dtype)


def forgetting_attention(q, k, v, fg_logit):
    S = q.shape[1]
    scale = 1.0 / math.sqrt(q.shape[-1])
    log_f = jax.nn.log_sigmoid(fg_logit.astype(jnp.float32))
    c = jnp.transpose(jnp.cumsum(log_f, axis=1), (0, 2, 1))
    outs = []
    for blk in range(S // Q_BLOCK):
        q0, q1 = blk * Q_BLOCK, (blk + 1) * Q_BLOCK
        qb = q[:, q0:q1]
        kb = k[:, :q1]
        vb = v[:, :q1]
        s = jnp.einsum('bqhd,bkhd->bhqk', qb, kb).astype(jnp.float32) * scale
        s = s + c[:, :, q0:q1, None] - c[:, :, None, :q1]
        causal = jnp.arange(q1)[None, :] <= jnp.arange(q0, q1)[:, None]
        s = jnp.where(causal, s, -jnp.inf)
        p = jax.nn.softmax(s, axis=-1).astype(v.dtype)
        outs.append(jnp.einsum('bhqk,bkhd->bqhd', p, vb))
    return jnp.concatenate(outs, axis=1)


def causal_pool_mixer(u, w_groups, scale):
    B, S, _ = u.shape
    ug = u.reshape(B, S, POOL_GROUPS, POOL_GROUP_DIM)
    u32 = ug.astype(jnp.float32)
    csum = jnp.cumsum(u32, axis=1)
    pos = jnp.arange(S)
    pooled = []
    for g, w in enumerate(POOL_WINDOWS):
        cg = csum[:, :, g]
        lag = jnp.concatenate([jnp.zeros((B, w, POOL_GROUP_DIM), jnp.float32), cg[:, :S - w]], axis=1)
        count = jnp.minimum(pos + 1, w).astype(jnp.float32)[None, :, None]
        pooled.append((cg - lag) / count - u32[:, :, g])
    pooled = jnp.stack(pooled, axis=2).astype(u.dtype)
    y = jnp.einsum('bsgc,gcd->bsgd', pooled, w_groups)
    return y.reshape(B, S, POOL_WIDTH) * scale


def memory_cross_attention(h, m, wq, wkv, wo):
    B, S, _ = h.shape
    q = (h @ wq).reshape(B, S, X_HEADS, X_HEAD_DIM)
    kv = m @ wkv
    k = kv[..., :D_MODEL].reshape(B, MEM_LEN, X_HEADS, X_HEAD_DIM)
    v = kv[..., D_MODEL:].reshape(B, MEM_LEN, X_HEADS, X_HEAD_DIM)
    s = jnp.einsum('bqhd,bkhd->bhqk', q, k).astype(jnp.float32) / math.sqrt(X_HEAD_DIM)
    p = jax.nn.softmax(s, axis=-1).astype(v.dtype)
    o = jnp.einsum('bhqk,bkhd->bqhd', p, v).reshape(B, S, D_MODEL)
    return o @ wo


def setup_inputs(seed: int = 0) -> dict:
    key = jax.random.key(seed)
    ks = jax.random.split(key, 20)
    f32 = jnp.float32

    def w(k, shape, fan_in):
        return jax.random.normal(k, shape, f32) * (fan_in ** -0.5)

    def gain(k):
        return 1.0 + 0.02 * jax.random.normal(k, (DEPTH, D_MODEL), f32)

    return {
        "x": jax.random.normal(ks[0], (BATCH, SEQ, D_MODEL), f32),
        "mem": jax.random.normal(ks[1], (BATCH, MEM_LEN, D_MODEL), f32),
        "g_mix_pre": gain(ks[2]),
        "w_in": w(ks[3], (DEPTH, D_MODEL, IN_COLS), D_MODEL),
        "b_forget": 2.0 + 0.1 * jax.random.normal(ks[4], (DEPTH, FOX_HEADS), f32),
        "pool_w": w(ks[5], (DEPTH, POOL_GROUPS, POOL_GROUP_DIM, POOL_GROUP_DIM), POOL_GROUP_DIM),
        "pool_scale": 1.0 + 0.02 * jax.random.normal(ks[6], (DEPTH, POOL_WIDTH), f32),
        "w_out": w(ks[7], (DEPTH, MIX_WIDTH, D_MODEL), MIX_WIDTH),
        "g_mix_post": gain(ks[8]),
        "g_x_pre": gain(ks[9]),
        "g_mem": gain(ks[10]),
        "wq_x": w(ks[11], (DEPTH, D_MODEL, D_MODEL), D_MODEL),
        "wkv_x": w(ks[12], (DEPTH, D_MODEL, 2 * D_MODEL), D_MODEL),
        "wo_x": w(ks[13], (DEPTH, D_MODEL, D_MODEL), D_MODEL),
        "g_x_post": gain(ks[14]),
        "g_ffn_pre": gain(ks[15]),
        "w_up": w(ks[16], (DEPTH, D_MODEL, D_FF), D_MODEL),
        "w_down": w(ks[17], (DEPTH, D_FF, D_MODEL), D_FF),
        "g_ffn_post": gain(ks[18]),
    }


def reference(x, mem, g_mix_pre, w_in, b_forget, pool_w, pool_scale, w_out, g_mix_post,
              g_x_pre, g_mem, wq_x, wkv_x, wo_x, g_x_post, g_ffn_pre, w_up, w_down, g_ffn_post):
    B, S, _ = x.shape
    o_q, o_k, o_v = 0, FOX_WIDTH, 2 * FOX_WIDTH
    o_f = 3 * FOX_WIDTH
    o_p = o_f + FOX_HEADS
    for l in range(DEPTH):
        h = rms_norm(x, g_mix_pre[l])
        proj = h @ w_in[l]
        q = proj[..., o_q:o_k].reshape(B, S, FOX_HEADS, FOX_HEAD_DIM)
        k = proj[..., o_k:o_v].reshape(B, S, FOX_HEADS, FOX_HEAD_DIM)
        v = proj[..., o_v:o_f].reshape(B, S, FOX_HEADS, FOX_HEAD_DIM)
        fg_logit = proj[..., o_f:o_p] + b_forget[l]
        u = proj[..., o_p:]
        attn_out = forgetting_attention(q, k, v, fg_logit).reshape(B, S, FOX_WIDTH)
        pool_out = causal_pool_mixer(u, pool_w[l], pool_scale[l])
        mix = jnp.concatenate([attn_out, pool_out], axis=-1) @ w_out[l]
        x = x + rms_norm(mix, g_mix_post[l])
        h = rms_norm(x, g_x_pre[l])
        m = rms_norm(mem, g_mem[l])
        xo = memory_cross_attention(h, m, wq_x[l], wkv_x[l], wo_x[l])
        x = x + rms_norm(xo, g_x_post[l])
        h = rms_norm(x, g_ffn_pre[l])
        a = jnp.square(jax.nn.relu(h @ w_up[l]))
        x = x + rms_norm(a @ w_down[l], g_ffn_post[l])
    return x
```

```python
import functools
import math

import jax
import jax.numpy as jnp
from jax import lax
from jax.experimental import pallas as pl
from jax.experimental.pallas import tpu as pltpu

D_MODEL = 1024
FOX_WIDTH = 512
FOX_HEAD_DIM = 64
FOX_HEADS = 8
HEAD_PAIRS = FOX_HEADS // 2
PAIR_WIDTH = 2 * FOX_HEAD_DIM
POOL_WIDTH = 512
POOL_WINDOWS = (2, 4, 8, 16)
POOL_GROUP_DIM = 128
POOL_HALO = 16
MEM_LEN = 256
X_HEADS = 4
X_HEAD_DIM = 256
D_FF = 4096
EPS = 1e-6

LANES = 128
TOKEN_TILE = 512
ATTN_TILE = 512
FF_CHUNK = 1024
VMEM_LIMIT = 48 * 1024 * 1024
MASK_VALUE = -1e30

_BF16 = jnp.bfloat16
_F32 = jnp.float32


def _rms(x, g):
    y = x * lax.rsqrt(jnp.mean(x * x, axis=-1, keepdims=True) + EPS)
    return y * g


def _dot(a, b):
    return jnp.dot(a, b, preferred_element_type=_F32)


def _dot_nt(a, b):
    return lax.dot_general(a, b, (((1,), (1,)), ((), ())), preferred_element_type=_F32)


def _const_spec(shape):
    nd = len(shape)
    return pl.BlockSpec(shape, lambda *_: (0,) * nd)


def _params(semantics):
    return pltpu.CompilerParams(dimension_semantics=semantics, vmem_limit_bytes=VMEM_LIMIT)


def _inproj_kernel(x_ref, g_ref, wq_ref, wk_ref, wv_ref, wf_ref, bf_ref, wu_ref,
                   q_ref, k_ref, v_ref, logf_ref, u_ref):
    h = _rms(x_ref[0], g_ref[...]).astype(_BF16)
    q = _dot(h, wq_ref[...]) * (1.0 / math.sqrt(FOX_HEAD_DIM))
    k = _dot(h, wk_ref[...])
    v = _dot(h, wv_ref[...])
    for p in range(HEAD_PAIRS):
        sl = slice(p * PAIR_WIDTH, (p + 1) * PAIR_WIDTH)
        q_ref[0, p] = q[:, sl].astype(_BF16)
        k_ref[0, p] = k[:, sl].astype(_BF16)
        v_ref[0, p] = v[:, sl].astype(_BF16)
    u_ref[0] = _dot(h, wu_ref[...])
    z = _dot_nt(wf_ref[...], h)[:FOX_HEADS] + bf_ref[...]
    logf_ref[0] = jnp.minimum(z, 0.0) - jnp.log(1.0 + jnp.exp(-jnp.abs(z)))


def _inproj(x, g, wq, wk, wv, wf, bf, wu):
    B, S, D = x.shape
    tm = TOKEN_TILE
    qkv_shape = jax.ShapeDtypeStruct((B, HEAD_PAIRS, S, PAIR_WIDTH), _BF16)
    qkv_spec = pl.BlockSpec((1, HEAD_PAIRS, tm, PAIR_WIDTH), lambda b, i: (b, 0, i, 0))
    return pl.pallas_call(
        _inproj_kernel,
        grid=(B, S // tm),
        in_specs=[
            pl.BlockSpec((1, tm, D), lambda b, i: (b, i, 0)),
            _const_spec(g.shape), _const_spec(wq.shape), _const_spec(wk.shape),
            _const_spec(wv.shape), _const_spec(wf.shape), _const_spec(bf.shape),
            _const_spec(wu.shape),
        ],
        out_specs=[
            qkv_spec, qkv_spec, qkv_spec,
            pl.BlockSpec((1, FOX_HEADS, tm), lambda b, i: (b, 0, i)),
            pl.BlockSpec((1, tm, POOL_WIDTH), lambda b, i: (b, i, 0)),
        ],
        out_shape=[
            qkv_shape, qkv_shape, qkv_shape,
            jax.ShapeDtypeStruct((B, FOX_HEADS, S), _F32),
            jax.ShapeDtypeStruct((B, S, POOL_WIDTH), _F32),
        ],
        compiler_params=_params(("parallel", "parallel")),
        name="inproj",
    )(x, g, wq, wk, wv, wf, bf, wu)


def _cumsum_kernel(x_ref, o_ref):
    rows, S = x_ref.shape
    r = lax.broadcasted_iota(jnp.int32, (LANES, LANES), 0)
    c = lax.broadcasted_iota(jnp.int32, (LANES, LANES), 1)
    tri = jnp.where(r <= c, 1.0, 0.0).astype(_BF16)
    carry = jnp.zeros((rows, 1), _F32)
    for j in range(S // LANES):
        x = x_ref[:, j * LANES:(j + 1) * LANES]
        hi = x.astype(_BF16)
        r1 = x - hi.astype(_F32)
        mid = r1.astype(_BF16)
        lo = (r1 - mid.astype(_F32)).astype(_BF16)
        cs = _dot(hi, tri) + _dot(mid, tri) + _dot(lo, tri)
        o_ref[:, j * LANES:(j + 1) * LANES] = cs + carry
        carry = carry + cs[:, LANES - 1:LANES]


def _cumsum(x):
    return pl.pallas_call(
        _cumsum_kernel,
        out_shape=jax.ShapeDtypeStruct(x.shape, _F32),
        compiler_params=pltpu.CompilerParams(vmem_limit_bytes=VMEM_LIMIT),
        name="gate_cumsum",
    )(x)


def _attn_kernel(q_ref, k_ref, v_ref, cq_ref, ck_ref, o_ref, m_ref, l_ref, acc_ref, *, t):
    i = pl.program_id(2)
    q = q_ref[0, 0]
    lane = lax.broadcasted_iota(jnp.int32, (1, PAIR_WIDTH), 1)
    first = lane < FOX_HEAD_DIM
    zero = jnp.zeros_like(q)
    qs = (jnp.where(first, q, zero), jnp.where(first, zero, q))
    cq = cq_ref[0, 0]
    m_ref[...] = jnp.full(m_ref.shape, MASK_VALUE, _F32)
    l_ref[...] = jnp.zeros(l_ref.shape, _F32)
    acc_ref[...] = jnp.zeros(acc_ref.shape, _F32)

    def tile(j, masked):
        start = pl.multiple_of(j * t, t)
        k = k_ref[0, 0, pl.ds(start, t), :]
        v = v_ref[0, 0, pl.ds(start, t), :]
        ck = ck_ref[0, 0, :, pl.ds(start, t)]
        pvs, alphas = [], []
        for h in range(2):
            s = _dot_nt(qs[h], k) + cq[:, h:h + 1] - ck[h:h + 1, :]
            if masked:
                row = lax.broadcasted_iota(jnp.int32, (t, t), 0)
                col = lax.broadcasted_iota(jnp.int32, (t, t), 1)
                s = jnp.where(col <= row, s, MASK_VALUE)
            m_old = m_ref[h]
            m_new = jnp.maximum(m_old, jnp.max(s, axis=-1, keepdims=True))
            alpha = jnp.exp(m_old - m_new)
            p = jnp.exp(s - m_new)
            l_ref[h] = alpha * l_ref[h] + jnp.sum(p, axis=-1, keepdims=True)
            m_ref[h] = m_new
            pvs.append(_dot(p.astype(_BF16), v))
            alphas.append(alpha)
        acc_ref[...] = (acc_ref[...] * jnp.where(first, alphas[0], alphas[1])
                        + jnp.where(first, pvs[0], pvs[1]))

    def body(j, carry):
        tile(j, False)
        return carry

    lax.fori_loop(0, i, body, 0)
    tile(i, True)
    inv = jnp.where(first, 1.0 / l_ref[0], 1.0 / l_ref[1])
    o_ref[0] = (acc_ref[...] * inv).astype(_BF16)


def _attention(q, k, v, cq, ck):
    B, _, S, _ = q.shape
    t = ATTN_TILE
    return pl.pallas_call(
        functools.partial(_attn_kernel, t=t),
        grid=(B, HEAD_PAIRS, S // t),
        in_specs=[
            pl.BlockSpec((1, 1, t, PAIR_WIDTH), lambda b, p, i: (b, p, i, 0)),
            pl.BlockSpec((1, 1, S, PAIR_WIDTH), lambda b, p, i: (b, p, 0, 0)),
            pl.BlockSpec((1, 1, S, PAIR_WIDTH), lambda b, p, i: (b, p, 0, 0)),
            pl.BlockSpec((1, 1, t, 2), lambda b, p, i: (b, p, i, 0)),
            pl.BlockSpec((1, 1, 2, S), lambda b, p, i: (b, p, 0, 0)),
        ],
        out_specs=pl.BlockSpec((1, t, PAIR_WIDTH), lambda b, p, i: (b, i, p)),
        out_shape=jax.ShapeDtypeStruct((B, S, FOX_WIDTH), _BF16),
        scratch_shapes=[
            pltpu.VMEM((2, t, 1), _F32),
            pltpu.VMEM((2, t, 1), _F32),
            pltpu.VMEM((t, PAIR_WIDTH), _F32),
        ],
        compiler_params=_params(("parallel", "parallel", "parallel")),
        name="fox_attention",
    )(q, k, v, cq, ck)


def _mixout_kernel(x_ref, attn_ref, u_ref, halo_ref, pw_ref, ps_ref, wo_ref, g_ref, o_ref):
    i = pl.program_id(1)
    tm = u_ref.shape[1]
    u = u_ref[0]
    halo = halo_ref[0] * (i > 0).astype(_F32)
    pos = i * tm + lax.broadcasted_iota(jnp.int32, (tm, 1), 0)
    outs = []
    for g, w in enumerate(POOL_WINDOWS):
        sl = slice(g * POOL_GROUP_DIM, (g + 1) * POOL_GROUP_DIM)
        ug = u[:, sl]
        ext = jnp.concatenate([halo[:, sl], ug], axis=0)
        d = 1
        while d < w:
            ext = ext[d:] + ext[:-d]
            d *= 2
        trailing = ext[ext.shape[0] - tm:]
        count = jnp.minimum(pos + 1, w).astype(_F32)
        pooled = (trailing / count - ug).astype(_BF16)
        outs.append(_dot(pooled, pw_ref[g]))
    pool = jnp.concatenate(outs, axis=1) * ps_ref[...]
    mix = (_dot(attn_ref[0], wo_ref[:FOX_WIDTH, :])
           + _dot(pool.astype(_BF16), wo_ref[FOX_WIDTH:, :]))
    o_ref[0] = x_ref[0] + _rms(mix, g_ref[...])


def _mixout(x, attn, u, pw, ps, wo, g):
    B, S, D = x.shape
    tm = TOKEN_TILE
    halo_blocks = tm // POOL_HALO
    return pl.pallas_call(
        _mixout_kernel,
        grid=(B, S // tm),
        in_specs=[
            pl.BlockSpec((1, tm, D), lambda b, i: (b, i, 0)),
            pl.BlockSpec((1, tm, FOX_WIDTH), lambda b, i: (b, i, 0)),
            pl.BlockSpec((1, tm, POOL_WIDTH), lambda b, i: (b, i, 0)),
            pl.BlockSpec((1, POOL_HALO, POOL_WIDTH),
                         lambda b, i: (b, jnp.maximum(i * halo_blocks - 1, 0), 0)),
            _const_spec(pw.shape), _const_spec(ps.shape), _const_spec(wo.shape),
            _const_spec(g.shape),
        ],
        out_specs=pl.BlockSpec((1, tm, D), lambda b, i: (b, i, 0)),
        out_shape=jax.ShapeDtypeStruct(x.shape, _F32),
        compiler_params=_params(("parallel", "parallel")),
        name="mix_out",
    )(x, attn, u, u, pw, ps, wo, g)


def _memkv_kernel(m_ref, g_ref, wk_ref, wv_ref, k_ref, v_ref):
    m = _rms(m_ref[0], g_ref[...]).astype(_BF16)
    k_ref[0] = _dot(m, wk_ref[...]).astype(_BF16)
    v_ref[0] = _dot(m, wv_ref[...]).astype(_BF16)


def _memkv(mem, g, wk, wv):
    B, M, D = mem.shape
    spec = pl.BlockSpec((1, M, D), lambda b: (b, 0, 0))
    shape = jax.ShapeDtypeStruct((B, M, D), _BF16)
    return pl.pallas_call(
        _memkv_kernel,
        grid=(B,),
        in_specs=[spec, _const_spec(g.shape), _const_spec(wk.shape), _const_spec(wv.shape)],
        out_specs=[spec, spec],
        out_shape=[shape, shape],
        compiler_params=_params(("parallel",)),
        name="mem_kv",
    )(mem, g, wk, wv)


def _xattn_kernel(x_ref, gpre_ref, wq_ref, k_ref, v_ref, wo_ref, gpost_ref, o_ref):
    x = x_ref[0]
    h = _rms(x, gpre_ref[...]).astype(_BF16)
    q = (_dot(h, wq_ref[...]) * (1.0 / math.sqrt(X_HEAD_DIM))).astype(_BF16)
    heads = []
    for hd in range(X_HEADS):
        sl = slice(hd * X_HEAD_DIM, (hd + 1) * X_HEAD_DIM)
        s = _dot_nt(q[:, sl], k_ref[0, :, sl])
        e = jnp.exp(s - jnp.max(s, axis=-1, keepdims=True))
        p = e / jnp.sum(e, axis=-1, keepdims=True)
        heads.append(_dot(p.astype(_BF16), v_ref[0, :, sl]).astype(_BF16))
    o = jnp.concatenate(heads, axis=1)
    o_ref[0] = x + _rms(_dot(o, wo_ref[...]), gpost_ref[...])


def _xattn(x, gpre, wq, k, v, wo, gpost):
    B, S, D = x.shape
    tm = TOKEN_TILE
    xspec = pl.BlockSpec((1, tm, D), lambda b, i: (b, i, 0))
    kvspec = pl.BlockSpec((1, MEM_LEN, D), lambda b, i: (b, 0, 0))
    return pl.pallas_call(
        _xattn_kernel,
        grid=(B, S // tm),
        in_specs=[xspec, _const_spec(gpre.shape), _const_spec(wq.shape), kvspec, kvspec,
                  _const_spec(wo.shape), _const_spec(gpost.shape)],
        out_specs=xspec,
        out_shape=jax.ShapeDtypeStruct(x.shape, _F32),
        compiler_params=_params(("parallel", "parallel")),
        name="mem_xattn",
    )(x, gpre, wq, k, v, wo, gpost)


def _mlp_kernel(x_ref, gpre_ref, wup_ref, wdown_ref, gpost_ref, o_ref):
    x = x_ref[...]
    h = _rms(x, gpre_ref[...]).astype(_BF16)
    y = jnp.zeros(x.shape, _F32)
    for c in range(D_FF // FF_CHUNK):
        sl = slice(c * FF_CHUNK, (c + 1) * FF_CHUNK)
        a = jnp.square(jnp.maximum(_dot(h, wup_ref[:, sl]), 0.0))
        y = y + _dot(a.astype(_BF16), wdown_ref[sl, :])
    o_ref[...] = x + _rms(y, gpost_ref[...])


def _mlp(x2d, gpre, wup, wdown, gpost):
    T, D = x2d.shape
    tm = TOKEN_TILE
    xspec = pl.BlockSpec((tm, D), lambda i: (i, 0))
    return pl.pallas_call(
        _mlp_kernel,
        grid=(T // tm,),
        in_specs=[xspec, _const_spec(gpre.shape), _const_spec(wup.shape),
                  _const_spec(wdown.shape), _const_spec(gpost.shape)],
        out_specs=xspec,
        out_shape=jax.ShapeDtypeStruct(x2d.shape, _F32),
        compiler_params=_params(("parallel",)),
        name="relu2_mlp",
    )(x2d, gpre, wup, wdown, gpost)


def kernel(x, mem, g_mix_pre, w_in, b_forget, pool_w, pool_scale, w_out, g_mix_post,
           g_x_pre, g_mem, wq_x, wkv_x, wo_x, g_x_post, g_ffn_pre, w_up, w_down, g_ffn_post):
    B, S, D = x.shape
    depth = w_in.shape[0]
    o_f = 3 * FOX_WIDTH
    o_p = o_f + FOX_HEADS
    row = lambda a: a.reshape(1, -1)
    for l in range(depth):
        w = w_in[l].astype(_BF16)
        wq, wk, wv = (w[:, j * FOX_WIDTH:(j + 1) * FOX_WIDTH] for j in range(3))
        wf = jnp.pad(w[:, o_f:o_p].T, ((0, 16 - FOX_HEADS), (0, 0)))
        wu = w[:, o_p:]
        q, k, v, logf, u = _inproj(x, row(g_mix_pre[l]), wq, wk, wv, wf,
                                   b_forget[l].reshape(FOX_HEADS, 1), wu)
        c = _cumsum(logf.reshape(B * FOX_HEADS, S)).reshape(B, HEAD_PAIRS, 2, S)
        attn = _attention(q, k, v, jnp.swapaxes(c, 2, 3), c)
        x = _mixout(x, attn, u, pool_w[l].astype(_BF16), row(pool_scale[l]),
                    w_out[l].astype(_BF16), row(g_mix_post[l]))
        wkv = wkv_x[l].astype(_BF16)
        mk, mv = _memkv(mem, row(g_mem[l]), wkv[:, :D], wkv[:, D:])
        x = _xattn(x, row(g_x_pre[l]), wq_x[l].astype(_BF16), mk, mv,
                   wo_x[l].astype(_BF16), row(g_x_post[l]))
        x = _mlp(x.reshape(B * S, D), row(g_ffn_pre[l]), w_up[l].astype(_BF16),
                 w_down[l].astype(_BF16), row(g_ffn_post[l])).reshape(B, S, D)
    return x
```

```python
import functools
import math

import jax
import jax.numpy as jnp
from jax import lax
from jax.experimental import pallas as pl
from jax.experimental.pallas import tpu as pltpu

D_MODEL = 1024
FOX_WIDTH = 512
FOX_HEAD_DIM = 64
FOX_HEADS = 8
HEAD_PAIRS = FOX_HEADS // 2
PAIR_WIDTH = 2 * FOX_HEAD_DIM
BIAS_PIECES = 3
POOL_WIDTH = 512
POOL_WINDOWS = (2, 4, 8, 16)
POOL_GROUP_DIM = 128
POOL_HALO = 16
MEM_LEN = 256
X_HEADS = 4
X_HEAD_DIM = 256
D_FF = 4096
EPS = 1e-6

LANES = 128
TOKEN_TILE = 512
ATTN_TILE = 512
FF_CHUNK = 1024
VMEM_LIMIT = 48 * 1024 * 1024
MASK_VALUE = -1e30

_BF16 = jnp.bfloat16
_F32 = jnp.float32


def _rms(x, g):
    y = x * lax.rsqrt(jnp.mean(x * x, axis=-1, keepdims=True) + EPS)
    return y * g


def _dot(a, b):
    return jnp.dot(a, b, preferred_element_type=_F32)


def _dot_nt(a, b):
    return lax.dot_general(a, b, (((1,), (1,)), ((), ())), preferred_element_type=_F32)


def _dot_tn(a, b):
    return lax.dot_general(a, b, (((0,), (0,)), ((), ())), preferred_element_type=_F32)


def _split3(x):
    hi = x.astype(_BF16).astype(_F32)
    r1 = x - hi
    mid = r1.astype(_BF16).astype(_F32)
    lo = (r1 - mid).astype(_BF16).astype(_F32)
    return hi, mid, lo


def _const_spec(shape):
    nd = len(shape)
    return pl.BlockSpec(shape, lambda *_: (0,) * nd)


def _params(semantics):
    return pltpu.CompilerParams(dimension_semantics=semantics, vmem_limit_bytes=VMEM_LIMIT)


def _inproj_kernel(x_ref, g_ref, wqt_ref, wk_ref, wvt_ref, wf_ref, bf_ref, wu_ref,
                   qt_ref, k_ref, vt_ref, logf_ref, u_ref):
    h = _rms(x_ref[0], g_ref[...]).astype(_BF16)
    qt_ref[0] = (_dot_nt(wqt_ref[...], h) * (1.0 / math.sqrt(FOX_HEAD_DIM))).astype(_BF16)
    vt_ref[0] = _dot_nt(wvt_ref[...], h).astype(_BF16)
    k = _dot(h, wk_ref[...])
    for p in range(HEAD_PAIRS):
        k_ref[0, p] = k[:, p * PAIR_WIDTH:(p + 1) * PAIR_WIDTH].astype(_BF16)
    u_ref[0] = _dot(h, wu_ref[...])
    z = _dot_nt(wf_ref[...], h)[:FOX_HEADS] + bf_ref[...]
    logf_ref[0] = jnp.minimum(z, 0.0) - jnp.log(1.0 + jnp.exp(-jnp.abs(z)))


def _inproj(x, g, wqt, wk, wvt, wf, bf, wu):
    B, S, D = x.shape
    tm = TOKEN_TILE
    t_shape = jax.ShapeDtypeStruct((B, FOX_WIDTH, S), _BF16)
    t_spec = pl.BlockSpec((1, FOX_WIDTH, tm), lambda b, i: (b, 0, i))
    return pl.pallas_call(
        _inproj_kernel,
        grid=(B, S // tm),
        in_specs=[
            pl.BlockSpec((1, tm, D), lambda b, i: (b, i, 0)),
            _const_spec(g.shape), _const_spec(wqt.shape), _const_spec(wk.shape),
            _const_spec(wvt.shape), _const_spec(wf.shape), _const_spec(bf.shape),
            _const_spec(wu.shape),
        ],
        out_specs=[
            t_spec,
            pl.BlockSpec((1, HEAD_PAIRS, tm, PAIR_WIDTH), lambda b, i: (b, 0, i, 0)),
            t_spec,
            pl.BlockSpec((1, FOX_HEADS, tm), lambda b, i: (b, 0, i)),
            pl.BlockSpec((1, tm, POOL_WIDTH), lambda b, i: (b, i, 0)),
        ],
        out_shape=[
            t_shape,
            jax.ShapeDtypeStruct((B, HEAD_PAIRS, S, PAIR_WIDTH), _BF16),
            t_shape,
            jax.ShapeDtypeStruct((B, FOX_HEADS, S), _F32),
            jax.ShapeDtypeStruct((B, S, POOL_WIDTH), _F32),
        ],
        compiler_params=_params(("parallel", "parallel")),
        name="inproj",
    )(x, g, wqt, wk, wvt, wf, bf, wu)


def _cumsum_kernel(x_ref, o_ref):
    rows, S = x_ref.shape
    r = lax.broadcasted_iota(jnp.int32, (LANES, LANES), 0)
    c = lax.broadcasted_iota(jnp.int32, (LANES, LANES), 1)
    tri = jnp.where(r <= c, 1.0, 0.0).astype(_BF16)
    carry = jnp.zeros((rows, 1), _F32)
    for j in range(S // LANES):
        hi, mid, lo = _split3(x_ref[:, j * LANES:(j + 1) * LANES])
        cs = (_dot(hi.astype(_BF16), tri) + _dot(mid.astype(_BF16), tri)
              + _dot(lo.astype(_BF16), tri))
        o_ref[:, j * LANES:(j + 1) * LANES] = cs + carry
        carry = carry + cs[:, LANES - 1:LANES]


def _cumsum(x):
    return pl.pallas_call(
        _cumsum_kernel,
        out_shape=jax.ShapeDtypeStruct(x.shape, _F32),
        compiler_params=pltpu.CompilerParams(vmem_limit_bytes=VMEM_LIMIT),
        name="gate_cumsum",
    )(x)


def _bias_lane_base(head):
    return FOX_HEAD_DIM if head % 2 == 0 else 0


def _keyaug_kernel(k_ref, ct_ref, o_ref):
    ts = k_ref.shape[2]
    lane = lax.broadcasted_iota(jnp.int32, (ts, PAIR_WIDTH), 1)
    pieces = _split3(ct_ref[0])
    for h in range(FOX_HEADS):
        base = _bias_lane_base(h)
        aug = jnp.where((lane >= base) & (lane < base + BIAS_PIECES), 1.0, 0.0)
        for p in range(BIAS_PIECES):
            aug = jnp.where(lane == base + BIAS_PIECES + p, -pieces[p][:, h:h + 1], aug)
        own = (lane < FOX_HEAD_DIM) == (h % 2 == 0)
        o_ref[0, h] = jnp.where(own, k_ref[0, h // 2], aug.astype(_BF16))


def _keyaug(k, ct):
    B, _, S, _ = k.shape
    ts = TOKEN_TILE
    return pl.pallas_call(
        _keyaug_kernel,
        grid=(B, S // ts),
        in_specs=[
            pl.BlockSpec((1, HEAD_PAIRS, ts, PAIR_WIDTH), lambda b, i: (b, 0, i, 0)),
            pl.BlockSpec((1, ts, FOX_HEADS), lambda b, i: (b, i, 0)),
        ],
        out_specs=pl.BlockSpec((1, FOX_HEADS, ts, PAIR_WIDTH), lambda b, i: (b, 0, i, 0)),
        out_shape=jax.ShapeDtypeStruct((B, FOX_HEADS, S, PAIR_WIDTH), _BF16),
        compiler_params=_params(("parallel", "parallel")),
        name="key_aug",
    )(k, ct)


def _attn_kernel(qt_ref, k_ref, vt_ref, c_ref, o_ref, q_scr, m_ref, l_ref, acc_ref, *, t):
    h = pl.program_id(1)
    i = pl.program_id(2)
    hi, mid, lo = _split3(c_ref[0, pl.ds(h, 1), :])
    row = lax.broadcasted_iota(jnp.int32, (FOX_HEAD_DIM, t), 0)
    aug = jnp.where(row < 2 * BIAS_PIECES, 1.0, 0.0)
    for p, piece in enumerate((hi, mid, lo)):
        aug = jnp.where(row == p, piece, aug)
    aug = aug.astype(_BF16)
    qt = qt_ref[0, 0]
    even = jnp.concatenate([qt, aug], axis=0)
    odd = jnp.concatenate([aug, qt], axis=0)
    q_scr[...] = jnp.where(h % 2 == 0, even, odd)
    m_ref[...] = jnp.full(m_ref.shape, MASK_VALUE, _F32)
    l_ref[...] = jnp.zeros(l_ref.shape, _F32)
    acc_ref[...] = jnp.zeros(acc_ref.shape, _F32)

    def tile(j, masked):
        start = pl.multiple_of(j * t, t)
        k = k_ref[0, 0, pl.ds(start, t), :]
        vt = vt_ref[0, 0, :, pl.ds(start, t)]
        st = _dot(k, q_scr[...])
        if masked:
            key = lax.broadcasted_iota(jnp.int32, (t, t), 0)
            qry = lax.broadcasted_iota(jnp.int32, (t, t), 1)
            st = jnp.where(key <= qry, st, MASK_VALUE)
        m_old = m_ref[...]
        m_new = jnp.maximum(m_old, jnp.max(st, axis=0, keepdims=True))
        alpha = jnp.exp(m_old - m_new)
        p = jnp.exp(st - m_new)
        l_ref[...] = alpha * l_ref[...] + jnp.sum(p, axis=0, keepdims=True)
        m_ref[...] = m_new
        acc_ref[...] = acc_ref[...] * alpha + _dot(vt, p.astype(_BF16))

    def body(j, carry):
        tile(j, False)
        return carry

    lax.fori_loop(0, i, body, 0)
    tile(i, True)
    o_ref[0, 0] = (acc_ref[...] * (1.0 / l_ref[...])).astype(_BF16)


def _attention(qt, k, vt, c):
    B, H, _, S = qt.shape
    t = ATTN_TILE
    return pl.pallas_call(
        functools.partial(_attn_kernel, t=t),
        grid=(B, H, S // t),
        in_specs=[
            pl.BlockSpec((1, 1, FOX_HEAD_DIM, t), lambda b, h, i: (b, h, 0, i)),
            pl.BlockSpec((1, 1, S, PAIR_WIDTH), lambda b, h, i: (b, h, 0, 0)),
            pl.BlockSpec((1, 1, FOX_HEAD_DIM, S), lambda b, h, i: (b, h, 0, 0)),
            pl.BlockSpec((1, H, t), lambda b, h, i: (b, 0, i)),
        ],
        out_specs=pl.BlockSpec((1, 1, FOX_HEAD_DIM, t), lambda b, h, i: (b, h, 0, i)),
        out_shape=jax.ShapeDtypeStruct((B, H, FOX_HEAD_DIM, S), _BF16),
        scratch_shapes=[
            pltpu.VMEM((PAIR_WIDTH, t), _BF16),
            pltpu.VMEM((1, t), _F32),
            pltpu.VMEM((1, t), _F32),
            pltpu.VMEM((FOX_HEAD_DIM, t), _F32),
        ],
        compiler_params=_params(("parallel", "parallel", "parallel")),
        name="fox_attention",
    )(qt, k, vt, c)


def _mixout_kernel(x_ref, attnt_ref, u_ref, halo_ref, pw_ref, ps_ref, wo_ref, g_ref, o_ref):
    i = pl.program_id(1)
    tm = u_ref.shape[1]
    u = u_ref[0]
    halo = halo_ref[0] * (i > 0).astype(_F32)
    pos = i * tm + lax.broadcasted_iota(jnp.int32, (tm, 1), 0)
    outs = []
    for g, w in enumerate(POOL_WINDOWS):
        sl = slice(g * POOL_GROUP_DIM, (g + 1) * POOL_GROUP_DIM)
        ug = u[:, sl]
        ext = jnp.concatenate([halo[:, sl], ug], axis=0)
        d = 1
        while d < w:
            ext = ext[d:] + ext[:-d]
            d *= 2
        trailing = ext[ext.shape[0] - tm:]
        count = jnp.minimum(pos + 1, w).astype(_F32)
        pooled = (trailing / count - ug).astype(_BF16)
        outs.append(_dot(pooled, pw_ref[g]))
    pool = jnp.concatenate(outs, axis=1) * ps_ref[...]
    mix = (_dot_tn(attnt_ref[0], wo_ref[:FOX_WIDTH, :])
           + _dot(pool.astype(_BF16), wo_ref[FOX_WIDTH:, :]))
    o_ref[0] = x_ref[0] + _rms(mix, g_ref[...])


def _mixout(x, attnt, u, pw, ps, wo, g):
    B, S, D = x.shape
    tm = TOKEN_TILE
    halo_blocks = tm // POOL_HALO
    return pl.pallas_call(
        _mixout_kernel,
        grid=(B, S // tm),
        in_specs=[
            pl.BlockSpec((1, tm, D), lambda b, i: (b, i, 0)),
            pl.BlockSpec((1, FOX_WIDTH, tm), lambda b, i: (b, 0, i)),
            pl.BlockSpec((1, tm, POOL_WIDTH), lambda b, i: (b, i, 0)),
            pl.BlockSpec((1, POOL_HALO, POOL_WIDTH),
                         lambda b, i: (b, jnp.maximum(i * halo_blocks - 1, 0), 0)),
            _const_spec(pw.shape), _const_spec(ps.shape), _const_spec(wo.shape),
            _const_spec(g.shape),
        ],
        out_specs=pl.BlockSpec((1, tm, D), lambda b, i: (b, i, 0)),
        out_shape=jax.ShapeDtypeStruct(x.shape, _F32),
        compiler_params=_params(("parallel", "parallel")),
        name="mix_out",
    )(x, attnt, u, u, pw, ps, wo, g)


def _memkv_kernel(m_ref, g_ref, wk_ref, wv_ref, k_ref, v_ref):
    m = _rms(m_ref[0], g_ref[...]).astype(_BF16)
    k_ref[0] = _dot(m, wk_ref[...]).astype(_BF16)
    v_ref[0] = _dot(m, wv_ref[...]).astype(_BF16)


def _memkv(mem, g, wk, wv):
    B, M, D = mem.shape
    spec = pl.BlockSpec((1, M, D), lambda b: (b, 0, 0))
    shape = jax.ShapeDtypeStruct((B, M, D), _BF16)
    return pl.pallas_call(
        _memkv_kernel,
        grid=(B,),
        in_specs=[spec, _const_spec(g.shape), _const_spec(wk.shape), _const_spec(wv.shape)],
        out_specs=[spec, spec],
        out_shape=[shape, shape],
        compiler_params=_params(("parallel",)),
        name="mem_kv",
    )(mem, g, wk, wv)


def _xattn_kernel(x_ref, gpre_ref, wq_ref, k_ref, v_ref, wo_ref, gpost_ref, o_ref):
    x = x_ref[0]
    h = _rms(x, gpre_ref[...]).astype(_BF16)
    q = (_dot(h, wq_ref[...]) * (1.0 / math.sqrt(X_HEAD_DIM))).astype(_BF16)
    heads = []
    for hd in range(X_HEADS):
        sl = slice(hd * X_HEAD_DIM, (hd + 1) * X_HEAD_DIM)
        s = _dot_nt(q[:, sl], k_ref[0, :, sl])
        e = jnp.exp(s - jnp.max(s, axis=-1, keepdims=True))
        p = e / jnp.sum(e, axis=-1, keepdims=True)
        heads.append(_dot(p.astype(_BF16), v_ref[0, :, sl]).astype(_BF16))
    o = jnp.concatenate(heads, axis=1)
    o_ref[0] = x + _rms(_dot(o, wo_ref[...]), gpost_ref[...])


def _xattn(x, gpre, wq, k, v, wo, gpost):
    B, S, D = x.shape
    tm = TOKEN_TILE
    xspec = pl.BlockSpec((1, tm, D), lambda b, i: (b, i, 0))
    kvspec = pl.BlockSpec((1, MEM_LEN, D), lambda b, i: (b, 0, 0))
    return pl.pallas_call(
        _xattn_kernel,
        grid=(B, S // tm),
        in_specs=[xspec, _const_spec(gpre.shape), _const_spec(wq.shape), kvspec, kvspec,
                  _const_spec(wo.shape), _const_spec(gpost.shape)],
        out_specs=xspec,
        out_shape=jax.ShapeDtypeStruct(x.shape, _F32),
        compiler_params=_params(("parallel", "parallel")),
        name="mem_xattn",
    )(x, gpre, wq, k, v, wo, gpost)


def _mlp_kernel(x_ref, gpre_ref, wup_ref, wdown_ref, gpost_ref, o_ref):
    x = x_ref[...]
    h = _rms(x, gpre_ref[...]).astype(_BF16)
    y = jnp.zeros(x.shape, _F32)
    for c in range(D_FF // FF_CHUNK):
        sl = slice(c * FF_CHUNK, (c + 1) * FF_CHUNK)
        a = jnp.square(jnp.maximum(_dot(h, wup_ref[:, sl]), 0.0))
        y = y + _dot(a.astype(_BF16), wdown_ref[sl, :])
    o_ref[...] = x + _rms(y, gpost_ref[...])


def _mlp(x2d, gpre, wup, wdown, gpost):
    T, D = x2d.shape
    tm = TOKEN_TILE
    xspec = pl.BlockSpec((tm, D), lambda i: (i, 0))
    return pl.pallas_call(
        _mlp_kernel,
        grid=(T // tm,),
        in_specs=[xspec, _const_spec(gpre.shape), _const_spec(wup.shape),
                  _const_spec(wdown.shape), _const_spec(gpost.shape)],
        out_specs=xspec,
        out_shape=jax.ShapeDtypeStruct(x2d.shape, _F32),
        compiler_params=_params(("parallel",)),
        name="relu2_mlp",
    )(x2d, gpre, wup, wdown, gpost)


def kernel(x, mem, g_mix_pre, w_in, b_forget, pool_w, pool_scale, w_out, g_mix_post,
           g_x_pre, g_mem, wq_x, wkv_x, wo_x, g_x_post, g_ffn_pre, w_up, w_down, g_ffn_post):
    B, S, D = x.shape
    depth = w_in.shape[0]
    o_f = 3 * FOX_WIDTH
    o_p = o_f + FOX_HEADS
    row = lambda a: a.reshape(1, -1)
    for l in range(depth):
        w = w_in[l].astype(_BF16)
        wq, wk, wv = (w[:, j * FOX_WIDTH:(j + 1) * FOX_WIDTH] for j in range(3))
        wf = jnp.pad(w[:, o_f:o_p].T, ((0, 16 - FOX_HEADS), (0, 0)))
        wu = w[:, o_p:]
        qt, k, vt, logf, u = _inproj(x, row(g_mix_pre[l]), wq.T, wk, wv.T, wf,
                                     b_forget[l].reshape(FOX_HEADS, 1), wu)
        c = _cumsum(logf.reshape(B * FOX_HEADS, S)).reshape(B, FOX_HEADS, S)
        kaug = _keyaug(k, jnp.swapaxes(c, 1, 2))
        per_head = (B, FOX_HEADS, FOX_HEAD_DIM, S)
        attnt = _attention(qt.reshape(per_head), kaug, vt.reshape(per_head), c)
        x = _mixout(x, attnt.reshape(B, FOX_WIDTH, S), u, pool_w[l].astype(_BF16),
                    row(pool_scale[l]), w_out[l].astype(_BF16), row(g_mix_post[l]))
        wkv = wkv_x[l].astype(_BF16)
        mk, mv = _memkv(mem, row(g_mem[l]), wkv[:, :D], wkv[:, D:])
        x = _xattn(x, row(g_x_pre[l]), wq_x[l].astype(_BF16), mk, mv,
                   wo_x[l].astype(_BF16), row(g_x_post[l]))
        x = _mlp(x.reshape(B * S, D), row(g_ffn_pre[l]), w_up[l].astype(_BF16),
                 w_down[l].astype(_BF16), row(g_ffn_post[l])).reshape(B, S, D)
    return x
```

```python
import functools
import math

import jax
import jax.numpy as jnp
from jax import lax
from jax.experimental import pallas as pl
from jax.experimental.pallas import tpu as pltpu

D_MODEL = 1024
FOX_WIDTH = 512
FOX_HEAD_DIM = 64
FOX_HEADS = 8
HEAD_PAIRS = FOX_HEADS // 2
PAIR_WIDTH = 2 * FOX_HEAD_DIM
DENOM_ROWS = 16
BIAS_PIECES = 3
POOL_WIDTH = 512
POOL_WINDOWS = (2, 4, 8, 16)
POOL_GROUP_DIM = 128
POOL_HALO = 16
MEM_LEN = 256
X_HEADS = 4
X_HEAD_DIM = 256
D_FF = 4096
EPS = 1e-6

LANES = 128
TOKEN_TILE = 512
ATTN_TILE = 512
FF_CHUNK = 1024
VMEM_LIMIT = 48 * 1024 * 1024
MASK_VALUE = -1e30
LOG2E = math.log2(math.e)

_BF16 = jnp.bfloat16
_F32 = jnp.float32


def _rms(x, g):
    y = x * lax.rsqrt(jnp.mean(x * x, axis=-1, keepdims=True) + EPS)
    return y * g


def _dot(a, b):
    return jnp.dot(a, b, preferred_element_type=_F32)


def _dot_nt(a, b):
    return lax.dot_general(a, b, (((1,), (1,)), ((), ())), preferred_element_type=_F32)


def _dot_tn(a, b):
    return lax.dot_general(a, b, (((0,), (0,)), ((), ())), preferred_element_type=_F32)


def _split3(x):
    hi = x.astype(_BF16).astype(_F32)
    r1 = x - hi
    mid = r1.astype(_BF16).astype(_F32)
    lo = (r1 - mid).astype(_BF16).astype(_F32)
    return hi, mid, lo


def _const_spec(shape):
    nd = len(shape)
    return pl.BlockSpec(shape, lambda *_: (0,) * nd)


def _params(semantics):
    return pltpu.CompilerParams(dimension_semantics=semantics, vmem_limit_bytes=VMEM_LIMIT)


def _inproj_kernel(x_ref, g_ref, wqt_ref, wk_ref, wvt_ref, wf_ref, bf_ref, wu_ref,
                   qt_ref, k_ref, vt_ref, logf_ref, u_ref):
    h = _rms(x_ref[0], g_ref[...]).astype(_BF16)
    qt_ref[0] = (_dot_nt(wqt_ref[...], h) * (LOG2E / math.sqrt(FOX_HEAD_DIM))).astype(_BF16)
    vt_ref[0] = _dot_nt(wvt_ref[...], h).astype(_BF16)
    k = _dot(h, wk_ref[...])
    for p in range(HEAD_PAIRS):
        k_ref[0, p] = k[:, p * PAIR_WIDTH:(p + 1) * PAIR_WIDTH].astype(_BF16)
    u_ref[0] = _dot(h, wu_ref[...])
    z = _dot_nt(wf_ref[...], h)[:FOX_HEADS] + bf_ref[...]
    logf_ref[0] = jnp.minimum(z, 0.0) - jnp.log(1.0 + jnp.exp(-jnp.abs(z)))


def _inproj(x, g, wqt, wk, wvt, wf, bf, wu):
    B, S, D = x.shape
    tm = TOKEN_TILE
    t_shape = jax.ShapeDtypeStruct((B, FOX_WIDTH, S), _BF16)
    t_spec = pl.BlockSpec((1, FOX_WIDTH, tm), lambda b, i: (b, 0, i))
    return pl.pallas_call(
        _inproj_kernel,
        grid=(B, S // tm),
        in_specs=[
            pl.BlockSpec((1, tm, D), lambda b, i: (b, i, 0)),
            _const_spec(g.shape), _const_spec(wqt.shape), _const_spec(wk.shape),
            _const_spec(wvt.shape), _const_spec(wf.shape), _const_spec(bf.shape),
            _const_spec(wu.shape),
        ],
        out_specs=[
            t_spec,
            pl.BlockSpec((1, HEAD_PAIRS, tm, PAIR_WIDTH), lambda b, i: (b, 0, i, 0)),
            t_spec,
            pl.BlockSpec((1, FOX_HEADS, tm), lambda b, i: (b, 0, i)),
            pl.BlockSpec((1, tm, POOL_WIDTH), lambda b, i: (b, i, 0)),
        ],
        out_shape=[
            t_shape,
            jax.ShapeDtypeStruct((B, HEAD_PAIRS, S, PAIR_WIDTH), _BF16),
            t_shape,
            jax.ShapeDtypeStruct((B, FOX_HEADS, S), _F32),
            jax.ShapeDtypeStruct((B, S, POOL_WIDTH), _F32),
        ],
        compiler_params=_params(("parallel", "parallel")),
        name="inproj",
    )(x, g, wqt, wk, wvt, wf, bf, wu)


def _cumsum_kernel(x_ref, o_ref):
    rows, S = x_ref.shape
    r = lax.broadcasted_iota(jnp.int32, (LANES, LANES), 0)
    c = lax.broadcasted_iota(jnp.int32, (LANES, LANES), 1)
    tri = jnp.where(r <= c, 1.0, 0.0).astype(_BF16)
    carry = jnp.zeros((rows, 1), _F32)
    for j in range(S // LANES):
        hi, mid, lo = _split3(x_ref[:, j * LANES:(j + 1) * LANES])
        cs = (_dot(hi.astype(_BF16), tri) + _dot(mid.astype(_BF16), tri)
              + _dot(lo.astype(_BF16), tri))
        o_ref[:, j * LANES:(j + 1) * LANES] = (cs + carry) * LOG2E
        carry = carry + cs[:, LANES - 1:LANES]


def _cumsum(x):
    return pl.pallas_call(
        _cumsum_kernel,
        out_shape=jax.ShapeDtypeStruct(x.shape, _F32),
        compiler_params=pltpu.CompilerParams(vmem_limit_bytes=VMEM_LIMIT),
        name="gate_cumsum",
    )(x)


def _bias_lane_base(head):
    return FOX_HEAD_DIM if head % 2 == 0 else 0


def _keyaug_kernel(k_ref, ct_ref, o_ref):
    ts = k_ref.shape[2]
    lane = lax.broadcasted_iota(jnp.int32, (ts, PAIR_WIDTH), 1)
    pieces = _split3(ct_ref[0])
    for h in range(FOX_HEADS):
        base = _bias_lane_base(h)
        aug = jnp.where((lane >= base) & (lane < base + BIAS_PIECES), 1.0, 0.0)
        for p in range(BIAS_PIECES):
            aug = jnp.where(lane == base + BIAS_PIECES + p, -pieces[p][:, h:h + 1], aug)
        own = (lane < FOX_HEAD_DIM) == (h % 2 == 0)
        o_ref[0, h] = jnp.where(own, k_ref[0, h // 2], aug.astype(_BF16))


def _keyaug(k, ct):
    B, _, S, _ = k.shape
    ts = TOKEN_TILE
    return pl.pallas_call(
        _keyaug_kernel,
        grid=(B, S // ts),
        in_specs=[
            pl.BlockSpec((1, HEAD_PAIRS, ts, PAIR_WIDTH), lambda b, i: (b, 0, i, 0)),
            pl.BlockSpec((1, ts, FOX_HEADS), lambda b, i: (b, i, 0)),
        ],
        out_specs=pl.BlockSpec((1, FOX_HEADS, ts, PAIR_WIDTH), lambda b, i: (b, 0, i, 0)),
        out_shape=jax.ShapeDtypeStruct((B, FOX_HEADS, S, PAIR_WIDTH), _BF16),
        compiler_params=_params(("parallel", "parallel")),
        name="key_aug",
    )(k, ct)


def _attn_kernel(qt_ref, k_ref, vt_ref, c_ref, o_ref,
                 q_scr, sa_ref, sb_ref, m_ref, acc_ref, *, t):
    h = pl.program_id(1)
    i = pl.program_id(2)
    hi, mid, lo = _split3(c_ref[0, pl.ds(h, 1), :])
    row = lax.broadcasted_iota(jnp.int32, (FOX_HEAD_DIM, t), 0)
    aug = jnp.where(row < 2 * BIAS_PIECES, 1.0, 0.0)
    for p, piece in enumerate((hi, mid, lo)):
        aug = jnp.where(row == p, piece, aug)
    aug = aug.astype(_BF16)
    qt = qt_ref[0, 0]
    even = jnp.concatenate([qt, aug], axis=0)
    odd = jnp.concatenate([aug, qt], axis=0)
    q_scr[...] = jnp.where(h % 2 == 0, even, odd)
    m_ref[...] = jnp.full(m_ref.shape, MASK_VALUE, _F32)
    acc_ref[...] = jnp.zeros(acc_ref.shape, _F32)

    def scores(j, dst):
        start = pl.multiple_of(j * t, t)
        dst[...] = _dot(k_ref[0, 0, pl.ds(start, t), :], q_scr[...])

    def consume(j, src, masked):
        start = pl.multiple_of(j * t, t)
        vt = jnp.concatenate([vt_ref[0, 0, :, pl.ds(start, t)],
                              jnp.ones((DENOM_ROWS, t), _BF16)], axis=0)
        st = src[...]
        if masked:
            key = lax.broadcasted_iota(jnp.int32, (t, t), 0)
            qry = lax.broadcasted_iota(jnp.int32, (t, t), 1)
            st = jnp.where(key <= qry, st, MASK_VALUE)
        m_old = m_ref[...]
        m_new = jnp.maximum(m_old, jnp.max(st, axis=0, keepdims=True))
        alpha = jnp.exp2(m_old - m_new)
        p = jnp.exp2(st - m_new).astype(_BF16)
        m_ref[...] = m_new
        acc_ref[...] = acc_ref[...] * alpha + _dot(vt, p)

    scores(0, sa_ref)

    def pair(jj, carry):
        j = 2 * jj
        scores(j + 1, sb_ref)
        consume(j, sa_ref, False)
        scores(j + 2, sa_ref)
        consume(j + 1, sb_ref, False)
        return carry

    lax.fori_loop(0, i // 2, pair, 0)

    @pl.when(i % 2 == 1)
    def _():
        scores(i, sb_ref)
        consume(i - 1, sa_ref, False)
        consume(i, sb_ref, True)

    @pl.when(i % 2 == 0)
    def _():
        consume(i, sa_ref, True)

    denom = acc_ref[FOX_HEAD_DIM:FOX_HEAD_DIM + 1, :]
    o_ref[0, 0] = (acc_ref[:FOX_HEAD_DIM, :] * (1.0 / denom)).astype(_BF16)


def _attention(qt, k, vt, c):
    B, H, _, S = qt.shape
    t = ATTN_TILE
    return pl.pallas_call(
        functools.partial(_attn_kernel, t=t),
        grid=(B, H, S // t),
        in_specs=[
            pl.BlockSpec((1, 1, FOX_HEAD_DIM, t), lambda b, h, i: (b, h, 0, i)),
            pl.BlockSpec((1, 1, S, PAIR_WIDTH), lambda b, h, i: (b, h, 0, 0)),
            pl.BlockSpec((1, 1, FOX_HEAD_DIM, S), lambda b, h, i: (b, h, 0, 0)),
            pl.BlockSpec((1, H, t), lambda b, h, i: (b, 0, i)),
        ],
        out_specs=pl.BlockSpec((1, 1, FOX_HEAD_DIM, t), lambda b, h, i: (b, h, 0, i)),
        out_shape=jax.ShapeDtypeStruct((B, H, FOX_HEAD_DIM, S), _BF16),
        scratch_shapes=[
            pltpu.VMEM((PAIR_WIDTH, t), _BF16),
            pltpu.VMEM((t, t), _F32),
            pltpu.VMEM((t, t), _F32),
            pltpu.VMEM((1, t), _F32),
            pltpu.VMEM((FOX_HEAD_DIM + DENOM_ROWS, t), _F32),
        ],
        compiler_params=_params(("parallel", "parallel", "parallel")),
        name="fox_attention",
    )(qt, k, vt, c)


def _mixout_kernel(x_ref, attnt_ref, u_ref, halo_ref, pw_ref, ps_ref, wo_ref, g_ref, o_ref):
    i = pl.program_id(1)
    tm = u_ref.shape[1]
    u = u_ref[0]
    halo = halo_ref[0] * (i > 0).astype(_F32)
    pos = i * tm + lax.broadcasted_iota(jnp.int32, (tm, 1), 0)
    outs = []
    for g, w in enumerate(POOL_WINDOWS):
        sl = slice(g * POOL_GROUP_DIM, (g + 1) * POOL_GROUP_DIM)
        ug = u[:, sl]
        ext = jnp.concatenate([halo[:, sl], ug], axis=0)
        d = 1
        while d < w:
            ext = ext[d:] + ext[:-d]
            d *= 2
        trailing = ext[ext.shape[0] - tm:]
        count = jnp.minimum(pos + 1, w).astype(_F32)
        pooled = (trailing / count - ug).astype(_BF16)
        outs.append(_dot(pooled, pw_ref[g]))
    pool = jnp.concatenate(outs, axis=1) * ps_ref[...]
    mix = (_dot_tn(attnt_ref[0], wo_ref[:FOX_WIDTH, :])
           + _dot(pool.astype(_BF16), wo_ref[FOX_WIDTH:, :]))
    o_ref[0] = x_ref[0] + _rms(mix, g_ref[...])


def _mixout(x, attnt, u, pw, ps, wo, g):
    B, S, D = x.shape
    tm = TOKEN_TILE
    halo_blocks = tm // POOL_HALO
    return pl.pallas_call(
        _mixout_kernel,
        grid=(B, S // tm),
        in_specs=[
            pl.BlockSpec((1, tm, D), lambda b, i: (b, i, 0)),
            pl.BlockSpec((1, FOX_WIDTH, tm), lambda b, i: (b, 0, i)),
            pl.BlockSpec((1, tm, POOL_WIDTH), lambda b, i: (b, i, 0)),
            pl.BlockSpec((1, POOL_HALO, POOL_WIDTH),
                         lambda b, i: (b, jnp.maximum(i * halo_blocks - 1, 0), 0)),
            _const_spec(pw.shape), _const_spec(ps.shape), _const_spec(wo.shape),
            _const_spec(g.shape),
        ],
        out_specs=pl.BlockSpec((1, tm, D), lambda b, i: (b, i, 0)),
        out_shape=jax.ShapeDtypeStruct(x.shape, _F32),
        compiler_params=_params(("parallel", "parallel")),
        name="mix_out",
    )(x, attnt, u, u, pw, ps, wo, g)


def _memkv_kernel(m_ref, g_ref, wk_ref, wv_ref, k_ref, v_ref):
    m = _rms(m_ref[0], g_ref[...]).astype(_BF16)
    k_ref[0] = _dot(m, wk_ref[...]).astype(_BF16)
    v_ref[0] = _dot(m, wv_ref[...]).astype(_BF16)


def _memkv(mem, g, wk, wv):
    B, M, D = mem.shape
    spec = pl.BlockSpec((1, M, D), lambda b: (b, 0, 0))
    shape = jax.ShapeDtypeStruct((B, M, D), _BF16)
    return pl.pallas_call(
        _memkv_kernel,
        grid=(B,),
        in_specs=[spec, _const_spec(g.shape), _const_spec(wk.shape), _const_spec(wv.shape)],
        out_specs=[spec, spec],
        out_shape=[shape, shape],
        compiler_params=_params(("parallel",)),
        name="mem_kv",
    )(mem, g, wk, wv)


def _xattn_kernel(x_ref, gpre_ref, wq_ref, k_ref, v_ref, wo_ref, gpost_ref, o_ref):
    x = x_ref[0]
    h = _rms(x, gpre_ref[...]).astype(_BF16)
    q = (_dot(h, wq_ref[...]) * (1.0 / math.sqrt(X_HEAD_DIM))).astype(_BF16)
    heads = []
    for hd in range(X_HEADS):
        sl = slice(hd * X_HEAD_DIM, (hd + 1) * X_HEAD_DIM)
        s = _dot_nt(q[:, sl], k_ref[0, :, sl])
        e = jnp.exp(s - jnp.max(s, axis=-1, keepdims=True))
        p = e / jnp.sum(e, axis=-1, keepdims=True)
        heads.append(_dot(p.astype(_BF16), v_ref[0, :, sl]).astype(_BF16))
    o = jnp.concatenate(heads, axis=1)
    o_ref[0] = x + _rms(_dot(o, wo_ref[...]), gpost_ref[...])


def _xattn(x, gpre, wq, k, v, wo, gpost):
    B, S, D = x.shape
    tm = TOKEN_TILE
    xspec = pl.BlockSpec((1, tm, D), lambda b, i: (b, i, 0))
    kvspec = pl.BlockSpec((1, MEM_LEN, D), lambda b, i: (b, 0, 0))
    return pl.pallas_call(
        _xattn_kernel,
        grid=(B, S // tm),
        in_specs=[xspec, _const_spec(gpre.shape), _const_spec(wq.shape), kvspec, kvspec,
                  _const_spec(wo.shape), _const_spec(gpost.shape)],
        out_specs=xspec,
        out_shape=jax.ShapeDtypeStruct(x.shape, _F32),
        compiler_params=_params(("parallel", "parallel")),
        name="mem_xattn",
    )(x, gpre, wq, k, v, wo, gpost)


def _mlp_kernel(x_ref, gpre_ref, wup_ref, wdown_ref, gpost_ref, o_ref):
    x = x_ref[...]
    h = _rms(x, gpre_ref[...]).astype(_BF16)
    y = jnp.zeros(x.shape, _F32)
    for c in range(D_FF // FF_CHUNK):
        sl = slice(c * FF_CHUNK, (c + 1) * FF_CHUNK)
        a = jnp.square(jnp.maximum(_dot(h, wup_ref[:, sl]), 0.0))
        y = y + _dot(a.astype(_BF16), wdown_ref[sl, :])
    o_ref[...] = x + _rms(y, gpost_ref[...])


def _mlp(x2d, gpre, wup, wdown, gpost):
    T, D = x2d.shape
    tm = TOKEN_TILE
    xspec = pl.BlockSpec((tm, D), lambda i: (i, 0))
    return pl.pallas_call(
        _mlp_kernel,
        grid=(T // tm,),
        in_specs=[xspec, _const_spec(gpre.shape), _const_spec(wup.shape),
                  _const_spec(wdown.shape), _const_spec(gpost.shape)],
        out_specs=xspec,
        out_shape=jax.ShapeDtypeStruct(x2d.shape, _F32),
        compiler_params=_params(("parallel",)),
        name="relu2_mlp",
    )(x2d, gpre, wup, wdown, gpost)


def kernel(x, mem, g_mix_pre, w_in, b_forget, pool_w, pool_scale, w_out, g_mix_post,
           g_x_pre, g_mem, wq_x, wkv_x, wo_x, g_x_post, g_ffn_pre, w_up, w_down, g_ffn_post):
    B, S, D = x.shape
    depth = w_in.shape[0]
    o_f = 3 * FOX_WIDTH
    o_p = o_f + FOX_HEADS
    row = lambda a: a.reshape(1, -1)
    for l in range(depth):
        w = w_in[l].astype(_BF16)
        wq, wk, wv = (w[:, j * FOX_WIDTH:(j + 1) * FOX_WIDTH] for j in range(3))
        wf = jnp.pad(w[:, o_f:o_p].T, ((0, 16 - FOX_HEADS), (0, 0)))
        wu = w[:, o_p:]
        qt, k, vt, logf, u = _inproj(x, row(g_mix_pre[l]), wq.T, wk, wv.T, wf,
                                     b_forget[l].reshape(FOX_HEADS, 1), wu)
        c = _cumsum(logf.reshape(B * FOX_HEADS, S)).reshape(B, FOX_HEADS, S)
        kaug = _keyaug(k, jnp.swapaxes(c, 1, 2))
        per_head = (B, FOX_HEADS, FOX_HEAD_DIM, S)
        attnt = _attention(qt.reshape(per_head), kaug, vt.reshape(per_head), c)
        x = _mixout(x, attnt.reshape(B, FOX_WIDTH, S), u, pool_w[l].astype(_BF16),
                    row(pool_scale[l]), w_out[l].astype(_BF16), row(g_mix_post[l]))
        wkv = wkv_x[l].astype(_BF16)
        mk, mv = _memkv(mem, row(g_mem[l]), wkv[:, :D], wkv[:, D:])
        x = _xattn(x, row(g_x_pre[l]), wq_x[l].astype(_BF16), mk, mv,
                   wo_x[l].astype(_BF16), row(g_x_post[l]))
        x = _mlp(x.reshape(B * S, D), row(g_ffn_pre[l]), w_up[l].astype(_BF16),
                 w_down[l].astype(_BF16), row(g_ffn_post[l])).reshape(B, S, D)
    return x
```

```python
import functools
import math

import jax
import jax.numpy as jnp
from jax import lax
from jax.experimental import pallas as pl
from jax.experimental.pallas import tpu as pltpu

D_MODEL = 1024
FOX_WIDTH = 512
FOX_HEAD_DIM = 64
FOX_HEADS = 8
HEAD_PAIRS = FOX_HEADS // 2
PAIR_WIDTH = 2 * FOX_HEAD_DIM
DENOM_ROWS = 16
BIAS_PIECES = 3
POOL_WIDTH = 512
POOL_WINDOWS = (2, 4, 8, 16)
POOL_GROUP_DIM = 128
POOL_HALO = 16
MEM_LEN = 256
X_HEADS = 4
X_HEAD_DIM = 256
D_FF = 4096
EPS = 1e-6

LANES = 128
TOKEN_TILE = 512
ATTN_TILE = 512
FF_CHUNK = 1024
VMEM_LIMIT = 48 * 1024 * 1024
MASK_VALUE = -1e30
LOG2E = math.log2(math.e)

_BF16 = jnp.bfloat16
_F32 = jnp.float32


def _rms(x, g):
    y = x * lax.rsqrt(jnp.mean(x * x, axis=-1, keepdims=True) + EPS)
    return y * g


def _dot(a, b):
    return jnp.dot(a, b, preferred_element_type=_F32)


def _dot_nt(a, b):
    return lax.dot_general(a, b, (((1,), (1,)), ((), ())), preferred_element_type=_F32)


def _dot_tn(a, b):
    return lax.dot_general(a, b, (((0,), (0,)), ((), ())), preferred_element_type=_F32)


def _split3(x):
    hi = x.astype(_BF16).astype(_F32)
    r1 = x - hi
    mid = r1.astype(_BF16).astype(_F32)
    lo = (r1 - mid).astype(_BF16).astype(_F32)
    return hi, mid, lo


def _const_spec(shape):
    nd = len(shape)
    return pl.BlockSpec(shape, lambda *_: (0,) * nd)


def _params(semantics):
    return pltpu.CompilerParams(dimension_semantics=semantics, vmem_limit_bytes=VMEM_LIMIT)


def _inproj_kernel(x_ref, g_ref, wqt_ref, wk_ref, wvt_ref, wf_ref, bf_ref, wu_ref,
                   qt_ref, k_ref, vt_ref, logf_ref, u_ref):
    h = _rms(x_ref[0], g_ref[...]).astype(_BF16)
    qt_ref[0] = (_dot_nt(wqt_ref[...], h) * (LOG2E / math.sqrt(FOX_HEAD_DIM))).astype(_BF16)
    vt_ref[0] = _dot_nt(wvt_ref[...], h).astype(_BF16)
    k = _dot(h, wk_ref[...])
    for p in range(HEAD_PAIRS):
        k_ref[0, p] = k[:, p * PAIR_WIDTH:(p + 1) * PAIR_WIDTH].astype(_BF16)
    u_ref[0] = _dot(h, wu_ref[...])
    z = _dot_nt(wf_ref[...], h)[:FOX_HEADS] + bf_ref[...]
    logf_ref[0] = jnp.minimum(z, 0.0) - jnp.log(1.0 + jnp.exp(-jnp.abs(z)))


def _inproj(x, g, wqt, wk, wvt, wf, bf, wu):
    B, S, D = x.shape
    tm = TOKEN_TILE
    t_shape = jax.ShapeDtypeStruct((B, FOX_WIDTH, S), _BF16)
    t_spec = pl.BlockSpec((1, FOX_WIDTH, tm), lambda b, i: (b, 0, i))
    return pl.pallas_call(
        _inproj_kernel,
        grid=(B, S // tm),
        in_specs=[
            pl.BlockSpec((1, tm, D), lambda b, i: (b, i, 0)),
            _const_spec(g.shape), _const_spec(wqt.shape), _const_spec(wk.shape),
            _const_spec(wvt.shape), _const_spec(wf.shape), _const_spec(bf.shape),
            _const_spec(wu.shape),
        ],
        out_specs=[
            t_spec,
            pl.BlockSpec((1, HEAD_PAIRS, tm, PAIR_WIDTH), lambda b, i: (b, 0, i, 0)),
            t_spec,
            pl.BlockSpec((1, FOX_HEADS, tm), lambda b, i: (b, 0, i)),
            pl.BlockSpec((1, tm, POOL_WIDTH), lambda b, i: (b, i, 0)),
        ],
        out_shape=[
            t_shape,
            jax.ShapeDtypeStruct((B, HEAD_PAIRS, S, PAIR_WIDTH), _BF16),
            t_shape,
            jax.ShapeDtypeStruct((B, FOX_HEADS, S), _F32),
            jax.ShapeDtypeStruct((B, S, POOL_WIDTH), _F32),
        ],
        compiler_params=_params(("parallel", "parallel")),
        name="inproj",
    )(x, g, wqt, wk, wvt, wf, bf, wu)


def _cumsum_kernel(x_ref, o_ref):
    rows, S = x_ref.shape
    r = lax.broadcasted_iota(jnp.int32, (LANES, LANES), 0)
    c = lax.broadcasted_iota(jnp.int32, (LANES, LANES), 1)
    tri = jnp.where(r <= c, 1.0, 0.0).astype(_BF16)
    carry = jnp.zeros((rows, 1), _F32)
    for j in range(S // LANES):
        hi, mid, lo = _split3(x_ref[:, j * LANES:(j + 1) * LANES])
        cs = (_dot(hi.astype(_BF16), tri) + _dot(mid.astype(_BF16), tri)
              + _dot(lo.astype(_BF16), tri))
        o_ref[:, j * LANES:(j + 1) * LANES] = (cs + carry) * LOG2E
        carry = carry + cs[:, LANES - 1:LANES]


def _cumsum(x):
    return pl.pallas_call(
        _cumsum_kernel,
        out_shape=jax.ShapeDtypeStruct(x.shape, _F32),
        compiler_params=pltpu.CompilerParams(vmem_limit_bytes=VMEM_LIMIT),
        name="gate_cumsum",
    )(x)


def _bias_lane_base(head):
    return FOX_HEAD_DIM if head % 2 == 0 else 0


def _keyaug_kernel(k_ref, ct_ref, o_ref):
    ts = k_ref.shape[2]
    lane = lax.broadcasted_iota(jnp.int32, (ts, PAIR_WIDTH), 1)
    pieces = _split3(ct_ref[0])
    for h in range(FOX_HEADS):
        base = _bias_lane_base(h)
        aug = jnp.where((lane >= base) & (lane < base + BIAS_PIECES), 1.0, 0.0)
        for p in range(BIAS_PIECES):
            aug = jnp.where(lane == base + BIAS_PIECES + p, -pieces[p][:, h:h + 1], aug)
        own = (lane < FOX_HEAD_DIM) == (h % 2 == 0)
        o_ref[0, h] = jnp.where(own, k_ref[0, h // 2], aug.astype(_BF16))


def _keyaug(k, ct):
    B, _, S, _ = k.shape
    ts = TOKEN_TILE
    return pl.pallas_call(
        _keyaug_kernel,
        grid=(B, S // ts),
        in_specs=[
            pl.BlockSpec((1, HEAD_PAIRS, ts, PAIR_WIDTH), lambda b, i: (b, 0, i, 0)),
            pl.BlockSpec((1, ts, FOX_HEADS), lambda b, i: (b, i, 0)),
        ],
        out_specs=pl.BlockSpec((1, FOX_HEADS, ts, PAIR_WIDTH), lambda b, i: (b, 0, i, 0)),
        out_shape=jax.ShapeDtypeStruct((B, FOX_HEADS, S, PAIR_WIDTH), _BF16),
        compiler_params=_params(("parallel", "parallel")),
        name="key_aug",
    )(k, ct)


SCORE_BUFFERS = 3
STATE_BUFFERS = 2


def _attn_kernel(qt_ref, k_ref, vt_ref, c_ref, o_ref, *scratch, t):
    q_scr = scratch[:STATE_BUFFERS]
    m_scr = scratch[STATE_BUFFERS:2 * STATE_BUFFERS]
    acc_scr = scratch[2 * STATE_BUFFERS:3 * STATE_BUFFERS]
    s_scr = scratch[3 * STATE_BUFFERS:]
    S = k_ref.shape[2]
    h = pl.program_id(1)
    tiles = [(i, j) for i in range(S // t) for j in range(i + 1)]

    def start_query_tile(i):
        slot = i % STATE_BUFFERS
        cols = slice(i * t, (i + 1) * t)
        hi, mid, lo = _split3(c_ref[0, pl.ds(h, 1), cols])
        row = lax.broadcasted_iota(jnp.int32, (FOX_HEAD_DIM, t), 0)
        aug = jnp.where(row < 2 * BIAS_PIECES, 1.0, 0.0)
        for p, piece in enumerate((hi, mid, lo)):
            aug = jnp.where(row == p, piece, aug)
        aug = aug.astype(_BF16)
        qt = qt_ref[0, 0, :, cols]
        even = jnp.concatenate([qt, aug], axis=0)
        odd = jnp.concatenate([aug, qt], axis=0)
        q_scr[slot][...] = jnp.where(h % 2 == 0, even, odd)
        m_scr[slot][...] = jnp.full((1, t), MASK_VALUE, _F32)
        acc_scr[slot][...] = jnp.zeros(acc_scr[slot].shape, _F32)

    def scores(n):
        i, j = tiles[n]
        s_scr[n % SCORE_BUFFERS][...] = _dot(k_ref[0, 0, j * t:(j + 1) * t, :],
                                             q_scr[i % STATE_BUFFERS][...])

    def consume(n):
        i, j = tiles[n]
        m_ref, acc_ref = m_scr[i % STATE_BUFFERS], acc_scr[i % STATE_BUFFERS]
        vt = jnp.concatenate([vt_ref[0, 0, :, j * t:(j + 1) * t],
                              jnp.ones((DENOM_ROWS, t), _BF16)], axis=0)
        st = s_scr[n % SCORE_BUFFERS][...]
        if j == i:
            key = lax.broadcasted_iota(jnp.int32, (t, t), 0)
            qry = lax.broadcasted_iota(jnp.int32, (t, t), 1)
            st = jnp.where(key <= qry, st, MASK_VALUE)
        m_old = m_ref[...]
        m_new = jnp.maximum(m_old, jnp.max(st, axis=0, keepdims=True))
        alpha = jnp.exp2(m_old - m_new)
        p = jnp.exp2(st - m_new).astype(_BF16)
        m_ref[...] = m_new
        acc = acc_ref[...] * alpha + _dot(vt, p)
        if j == i:
            denom = acc[FOX_HEAD_DIM:FOX_HEAD_DIM + 1, :]
            o_ref[0, 0, :, i * t:(i + 1) * t] = (acc[:FOX_HEAD_DIM, :] * (1.0 / denom)).astype(_BF16)
        else:
            acc_ref[...] = acc

    start_query_tile(0)
    scores(0)
    for n in range(len(tiles)):
        if n + 1 < len(tiles):
            if tiles[n + 1][1] == 0:
                start_query_tile(tiles[n + 1][0])
            scores(n + 1)
        consume(n)


def _attention(qt, k, vt, c):
    B, H, _, S = qt.shape
    t = ATTN_TILE
    head_rows = pl.BlockSpec((1, 1, FOX_HEAD_DIM, S), lambda b, h: (b, h, 0, 0))
    return pl.pallas_call(
        functools.partial(_attn_kernel, t=t),
        grid=(B, H),
        in_specs=[
            head_rows,
            pl.BlockSpec((1, 1, S, PAIR_WIDTH), lambda b, h: (b, h, 0, 0)),
            head_rows,
            pl.BlockSpec((1, H, S), lambda b, h: (b, 0, 0)),
        ],
        out_specs=head_rows,
        out_shape=jax.ShapeDtypeStruct((B, H, FOX_HEAD_DIM, S), _BF16),
        scratch_shapes=(
            [pltpu.VMEM((PAIR_WIDTH, t), _BF16)] * STATE_BUFFERS
            + [pltpu.VMEM((1, t), _F32)] * STATE_BUFFERS
            + [pltpu.VMEM((FOX_HEAD_DIM + DENOM_ROWS, t), _F32)] * STATE_BUFFERS
            + [pltpu.VMEM((t, t), _F32)] * SCORE_BUFFERS
        ),
        compiler_params=_params(("parallel", "parallel")),
        name="fox_attention",
    )(qt, k, vt, c)


def _memkv_kernel(m_ref, g_ref, wk_ref, wv_ref, k_ref, v_ref):
    m = _rms(m_ref[0], g_ref[...]).astype(_BF16)
    k_ref[0] = _dot(m, wk_ref[...]).astype(_BF16)
    v_ref[0] = _dot(m, wv_ref[...]).astype(_BF16)


def _memkv(mem, g, wk, wv):
    B, M, D = mem.shape
    spec = pl.BlockSpec((1, M, D), lambda b: (b, 0, 0))
    shape = jax.ShapeDtypeStruct((B, M, D), _BF16)
    return pl.pallas_call(
        _memkv_kernel,
        grid=(B,),
        in_specs=[spec, _const_spec(g.shape), _const_spec(wk.shape), _const_spec(wv.shape)],
        out_specs=[spec, spec],
        out_shape=[shape, shape],
        compiler_params=_params(("parallel",)),
        name="mem_kv",
    )(mem, g, wk, wv)


def _pool_mix(u, halo, first_row, pw_ref, ps_ref):
    tm = u.shape[0]
    pos = first_row + lax.broadcasted_iota(jnp.int32, (tm, 1), 0)
    outs = []
    for g, w in enumerate(POOL_WINDOWS):
        sl = slice(g * POOL_GROUP_DIM, (g + 1) * POOL_GROUP_DIM)
        ug = u[:, sl]
        ext = jnp.concatenate([halo[:, sl], ug], axis=0)
        d = 1
        while d < w:
            ext = ext[d:] + ext[:-d]
            d *= 2
        trailing = ext[ext.shape[0] - tm:]
        count = jnp.minimum(pos + 1, w).astype(_F32)
        pooled = (trailing / count - ug).astype(_BF16)
        outs.append(_dot(pooled, pw_ref[g]))
    return jnp.concatenate(outs, axis=1) * ps_ref[...]


def _memory_attention(h, wq_ref, k_ref, v_ref, wo_ref):
    q = (_dot(h, wq_ref[...]) * (1.0 / math.sqrt(X_HEAD_DIM))).astype(_BF16)
    heads = []
    for hd in range(X_HEADS):
        sl = slice(hd * X_HEAD_DIM, (hd + 1) * X_HEAD_DIM)
        s = _dot_nt(q[:, sl], k_ref[0, :, sl])
        e = jnp.exp(s - jnp.max(s, axis=-1, keepdims=True))
        p = e / jnp.sum(e, axis=-1, keepdims=True)
        heads.append(_dot(p.astype(_BF16), v_ref[0, :, sl]).astype(_BF16))
    return _dot(jnp.concatenate(heads, axis=1), wo_ref[...])


def _mix_xattn_kernel(x_ref, attnt_ref, u_ref, halo_ref, pw_ref, ps_ref, wo_ref, gmix_ref,
                      gpre_ref, wq_ref, k_ref, v_ref, wox_ref, gpost_ref, o_ref):
    i = pl.program_id(1)
    tm = u_ref.shape[1]
    halo = halo_ref[0] * (i > 0).astype(_F32)
    pool = _pool_mix(u_ref[0], halo, i * tm, pw_ref, ps_ref)
    mix = (_dot_tn(attnt_ref[0], wo_ref[:FOX_WIDTH, :])
           + _dot(pool.astype(_BF16), wo_ref[FOX_WIDTH:, :]))
    x = x_ref[0] + _rms(mix, gmix_ref[...])
    h = _rms(x, gpre_ref[...]).astype(_BF16)
    o_ref[0] = x + _rms(_memory_attention(h, wq_ref, k_ref, v_ref, wox_ref), gpost_ref[...])


def _mix_xattn(x, attnt, u, pw, ps, wo, gmix, gpre, wq, mk, mv, wox, gpost):
    B, S, D = x.shape
    tm = TOKEN_TILE
    halo_blocks = tm // POOL_HALO
    xspec = pl.BlockSpec((1, tm, D), lambda b, i: (b, i, 0))
    kvspec = pl.BlockSpec((1, MEM_LEN, D), lambda b, i: (b, 0, 0))
    consts = (pw, ps, wo, gmix, gpre, wq)
    return pl.pallas_call(
        _mix_xattn_kernel,
        grid=(B, S // tm),
        in_specs=[
            xspec,
            pl.BlockSpec((1, FOX_WIDTH, tm), lambda b, i: (b, 0, i)),
            pl.BlockSpec((1, tm, POOL_WIDTH), lambda b, i: (b, i, 0)),
            pl.BlockSpec((1, POOL_HALO, POOL_WIDTH),
                         lambda b, i: (b, jnp.maximum(i * halo_blocks - 1, 0), 0)),
            *[_const_spec(a.shape) for a in consts],
            kvspec, kvspec, _const_spec(wox.shape), _const_spec(gpost.shape),
        ],
        out_specs=xspec,
        out_shape=jax.ShapeDtypeStruct(x.shape, _F32),
        compiler_params=_params(("parallel", "parallel")),
        name="mix_xattn",
    )(x, attnt, u, u, *consts, mk, mv, wox, gpost)


def _mlp_kernel(x_ref, gpre_ref, wup_ref, wdown_ref, gpost_ref, o_ref):
    x = x_ref[...]
    h = _rms(x, gpre_ref[...]).astype(_BF16)
    y = jnp.zeros(x.shape, _F32)
    for c in range(D_FF // FF_CHUNK):
        sl = slice(c * FF_CHUNK, (c + 1) * FF_CHUNK)
        a = jnp.square(jnp.maximum(_dot(h, wup_ref[:, sl]), 0.0))
        y = y + _dot(a.astype(_BF16), wdown_ref[sl, :])
    o_ref[...] = x + _rms(y, gpost_ref[...])


def _mlp(x2d, gpre, wup, wdown, gpost):
    T, D = x2d.shape
    tm = TOKEN_TILE
    xspec = pl.BlockSpec((tm, D), lambda i: (i, 0))
    return pl.pallas_call(
        _mlp_kernel,
        grid=(T // tm,),
        in_specs=[xspec, _const_spec(gpre.shape), _const_spec(wup.shape),
                  _const_spec(wdown.shape), _const_spec(gpost.shape)],
        out_specs=xspec,
        out_shape=jax.ShapeDtypeStruct(x2d.shape, _F32),
        compiler_params=_params(("parallel",)),
        name="relu2_mlp",
    )(x2d, gpre, wup, wdown, gpost)


def kernel(x, mem, g_mix_pre, w_in, b_forget, pool_w, pool_scale, w_out, g_mix_post,
           g_x_pre, g_mem, wq_x, wkv_x, wo_x, g_x_post, g_ffn_pre, w_up, w_down, g_ffn_post):
    B, S, D = x.shape
    depth = w_in.shape[0]
    o_f = 3 * FOX_WIDTH
    o_p = o_f + FOX_HEADS
    row = lambda a: a.reshape(1, -1)
    for l in range(depth):
        w = w_in[l].astype(_BF16)
        wq, wk, wv = (w[:, j * FOX_WIDTH:(j + 1) * FOX_WIDTH] for j in range(3))
        wf = jnp.pad(w[:, o_f:o_p].T, ((0, 16 - FOX_HEADS), (0, 0)))
        wu = w[:, o_p:]
        qt, k, vt, logf, u = _inproj(x, row(g_mix_pre[l]), wq.T, wk, wv.T, wf,
                                     b_forget[l].reshape(FOX_HEADS, 1), wu)
        c = _cumsum(logf.reshape(B * FOX_HEADS, S)).reshape(B, FOX_HEADS, S)
        kaug = _keyaug(k, jnp.swapaxes(c, 1, 2))
        per_head = (B, FOX_HEADS, FOX_HEAD_DIM, S)
        attnt = _attention(qt.reshape(per_head), kaug, vt.reshape(per_head), c)
        wkv = wkv_x[l].astype(_BF16)
        mk, mv = _memkv(mem, row(g_mem[l]), wkv[:, :D], wkv[:, D:])
        x = _mix_xattn(x, attnt.reshape(B, FOX_WIDTH, S), u, pool_w[l].astype(_BF16),
                       row(pool_scale[l]), w_out[l].astype(_BF16), row(g_mix_post[l]),
                       row(g_x_pre[l]), wq_x[l].astype(_BF16), mk, mv,
                       wo_x[l].astype(_BF16), row(g_x_post[l]))
        x = _mlp(x.reshape(B * S, D), row(g_ffn_pre[l]), w_up[l].astype(_BF16),
                 w_down[l].astype(_BF16), row(g_ffn_post[l])).reshape(B, S, D)
    return x
```

```python
import functools
import math

import jax
import jax.numpy as jnp
from jax import lax
from jax.experimental import pallas as pl
from jax.experimental.pallas import tpu as pltpu

D_MODEL = 1024
FOX_WIDTH = 512
FOX_HEAD_DIM = 64
FOX_HEADS = 8
HEAD_PAIRS = FOX_HEADS // 2
PAIR_WIDTH = 2 * FOX_HEAD_DIM
DENOM_ROWS = 16
BIAS_PIECES = 3
POOL_WIDTH = 512
POOL_WINDOWS = (2, 4, 8, 16)
POOL_GROUP_DIM = 128
POOL_HALO = 16
MEM_LEN = 256
X_HEADS = 4
X_HEAD_DIM = 256
D_FF = 4096
EPS = 1e-6

LANES = 128
TOKEN_TILE = 512
ATTN_TILE = 512
MIX_CHAINS = 2
FF_CHUNK = 1024
VMEM_LIMIT = 48 * 1024 * 1024
MASK_VALUE = -1e30
LOG2E = math.log2(math.e)

_BF16 = jnp.bfloat16
_F32 = jnp.float32


def _rms(x, g):
    y = x * lax.rsqrt(jnp.mean(x * x, axis=-1, keepdims=True) + EPS)
    return y * g


def _dot(a, b):
    return jnp.dot(a, b, preferred_element_type=_F32)


def _dot_nt(a, b):
    return lax.dot_general(a, b, (((1,), (1,)), ((), ())), preferred_element_type=_F32)


def _dot_tn(a, b):
    return lax.dot_general(a, b, (((0,), (0,)), ((), ())), preferred_element_type=_F32)


def _split3(x):
    hi = x.astype(_BF16).astype(_F32)
    r1 = x - hi
    mid = r1.astype(_BF16).astype(_F32)
    lo = (r1 - mid).astype(_BF16).astype(_F32)
    return hi, mid, lo


def _const_spec(shape):
    nd = len(shape)
    return pl.BlockSpec(shape, lambda *_: (0,) * nd)


def _params(semantics):
    return pltpu.CompilerParams(dimension_semantics=semantics, vmem_limit_bytes=VMEM_LIMIT)


def _inproj_kernel(x_ref, g_ref, wqt_ref, wk_ref, wvt_ref, wf_ref, bf_ref, wu_ref,
                   qt_ref, k_ref, vt_ref, logf_ref, u_ref):
    h = _rms(x_ref[0], g_ref[...]).astype(_BF16)
    qt_ref[0] = (_dot_nt(wqt_ref[...], h) * (LOG2E / math.sqrt(FOX_HEAD_DIM))).astype(_BF16)
    vt_ref[0] = _dot_nt(wvt_ref[...], h).astype(_BF16)
    k = _dot(h, wk_ref[...])
    for p in range(HEAD_PAIRS):
        k_ref[0, p] = k[:, p * PAIR_WIDTH:(p + 1) * PAIR_WIDTH].astype(_BF16)
    u_ref[0] = _dot(h, wu_ref[...])
    z = _dot_nt(wf_ref[...], h)[:FOX_HEADS] + bf_ref[...]
    logf_ref[0] = jnp.minimum(z, 0.0) - jnp.log(1.0 + jnp.exp(-jnp.abs(z)))


def _inproj(x, g, wqt, wk, wvt, wf, bf, wu):
    B, S, D = x.shape
    tm = TOKEN_TILE
    t_shape = jax.ShapeDtypeStruct((B, FOX_WIDTH, S), _BF16)
    t_spec = pl.BlockSpec((1, FOX_WIDTH, tm), lambda b, i: (b, 0, i))
    return pl.pallas_call(
        _inproj_kernel,
        grid=(B, S // tm),
        in_specs=[
            pl.BlockSpec((1, tm, D), lambda b, i: (b, i, 0)),
            _const_spec(g.shape), _const_spec(wqt.shape), _const_spec(wk.shape),
            _const_spec(wvt.shape), _const_spec(wf.shape), _const_spec(bf.shape),
            _const_spec(wu.shape),
        ],
        out_specs=[
            t_spec,
            pl.BlockSpec((1, HEAD_PAIRS, tm, PAIR_WIDTH), lambda b, i: (b, 0, i, 0)),
            t_spec,
            pl.BlockSpec((1, FOX_HEADS, tm), lambda b, i: (b, 0, i)),
            pl.BlockSpec((1, tm, POOL_WIDTH), lambda b, i: (b, i, 0)),
        ],
        out_shape=[
            t_shape,
            jax.ShapeDtypeStruct((B, HEAD_PAIRS, S, PAIR_WIDTH), _BF16),
            t_shape,
            jax.ShapeDtypeStruct((B, FOX_HEADS, S), _F32),
            jax.ShapeDtypeStruct((B, S, POOL_WIDTH), _F32),
        ],
        compiler_params=_params(("parallel", "parallel")),
        name="inproj",
    )(x, g, wqt, wk, wvt, wf, bf, wu)


def _cumsum_kernel(x_ref, o_ref):
    rows, S = x_ref.shape
    r = lax.broadcasted_iota(jnp.int32, (LANES, LANES), 0)
    c = lax.broadcasted_iota(jnp.int32, (LANES, LANES), 1)
    tri = jnp.where(r <= c, 1.0, 0.0).astype(_BF16)
    carry = jnp.zeros((rows, 1), _F32)
    for j in range(S // LANES):
        hi, mid, lo = _split3(x_ref[:, j * LANES:(j + 1) * LANES])
        cs = (_dot(hi.astype(_BF16), tri) + _dot(mid.astype(_BF16), tri)
              + _dot(lo.astype(_BF16), tri))
        o_ref[:, j * LANES:(j + 1) * LANES] = (cs + carry) * LOG2E
        carry = carry + cs[:, LANES - 1:LANES]


def _cumsum(x):
    return pl.pallas_call(
        _cumsum_kernel,
        out_shape=jax.ShapeDtypeStruct(x.shape, _F32),
        compiler_params=pltpu.CompilerParams(vmem_limit_bytes=VMEM_LIMIT),
        name="gate_cumsum",
    )(x)


def _bias_lane_base(head):
    return FOX_HEAD_DIM if head % 2 == 0 else 0


def _keyaug_kernel(k_ref, ct_ref, o_ref):
    ts = k_ref.shape[2]
    lane = lax.broadcasted_iota(jnp.int32, (ts, PAIR_WIDTH), 1)
    pieces = _split3(ct_ref[0])
    for h in range(FOX_HEADS):
        base = _bias_lane_base(h)
        aug = jnp.where((lane >= base) & (lane < base + BIAS_PIECES), 1.0, 0.0)
        for p in range(BIAS_PIECES):
            aug = jnp.where(lane == base + BIAS_PIECES + p, -pieces[p][:, h:h + 1], aug)
        own = (lane < FOX_HEAD_DIM) == (h % 2 == 0)
        o_ref[0, h] = jnp.where(own, k_ref[0, h // 2], aug.astype(_BF16))


def _keyaug(k, ct):
    B, _, S, _ = k.shape
    ts = TOKEN_TILE
    return pl.pallas_call(
        _keyaug_kernel,
        grid=(B, S // ts),
        in_specs=[
            pl.BlockSpec((1, HEAD_PAIRS, ts, PAIR_WIDTH), lambda b, i: (b, 0, i, 0)),
            pl.BlockSpec((1, ts, FOX_HEADS), lambda b, i: (b, i, 0)),
        ],
        out_specs=pl.BlockSpec((1, FOX_HEADS, ts, PAIR_WIDTH), lambda b, i: (b, 0, i, 0)),
        out_shape=jax.ShapeDtypeStruct((B, FOX_HEADS, S, PAIR_WIDTH), _BF16),
        compiler_params=_params(("parallel", "parallel")),
        name="key_aug",
    )(k, ct)


SCORE_LEAD = 2
STATE_BUFFERS = 2


def _attn_kernel(qt_ref, k_ref, vt_ref, c_ref, o_ref, *scratch, t):
    q_scr = scratch[:STATE_BUFFERS]
    m_scr = scratch[STATE_BUFFERS:2 * STATE_BUFFERS]
    acc_scr = scratch[2 * STATE_BUFFERS:]
    S = k_ref.shape[2]
    h = pl.program_id(1)
    tiles = [(i, j) for i in range(S // t) for j in range(i + 1)]

    def start_query_tile(i):
        slot = i % STATE_BUFFERS
        cols = slice(i * t, (i + 1) * t)
        hi, mid, lo = _split3(c_ref[0, pl.ds(h, 1), cols])
        row = lax.broadcasted_iota(jnp.int32, (FOX_HEAD_DIM, t), 0)
        aug = jnp.where(row < 2 * BIAS_PIECES, 1.0, 0.0)
        for p, piece in enumerate((hi, mid, lo)):
            aug = jnp.where(row == p, piece, aug)
        aug = aug.astype(_BF16)
        qt = qt_ref[0, 0, :, cols]
        even = jnp.concatenate([qt, aug], axis=0)
        odd = jnp.concatenate([aug, qt], axis=0)
        q_scr[slot][...] = jnp.where(h % 2 == 0, even, odd)
        m_scr[slot][...] = jnp.full((1, t), MASK_VALUE, _F32)
        acc_scr[slot][...] = jnp.zeros(acc_scr[slot].shape, _F32)

    def scores(n):
        i, j = tiles[n]
        return _dot(k_ref[0, 0, j * t:(j + 1) * t, :],
                    q_scr[i % STATE_BUFFERS][...])

    def consume(n, st):
        i, j = tiles[n]
        m_ref, acc_ref = m_scr[i % STATE_BUFFERS], acc_scr[i % STATE_BUFFERS]
        vt = jnp.concatenate([vt_ref[0, 0, :, j * t:(j + 1) * t],
                              jnp.ones((DENOM_ROWS, t), _BF16)], axis=0)
        if j == i:
            key = lax.broadcasted_iota(jnp.int32, (t, t), 0)
            qry = lax.broadcasted_iota(jnp.int32, (t, t), 1)
            st = jnp.where(key <= qry, st, MASK_VALUE)
        m_old = m_ref[...]
        m_new = jnp.maximum(m_old, jnp.max(st, axis=0, keepdims=True))
        alpha = jnp.exp2(m_old - m_new)
        p = jnp.exp2(st - m_new).astype(_BF16)
        m_ref[...] = m_new
        acc = acc_ref[...] * alpha + _dot(vt, p)
        if j == i:
            denom = acc[FOX_HEAD_DIM:FOX_HEAD_DIM + 1, :]
            o_ref[0, 0, :, i * t:(i + 1) * t] = (acc[:FOX_HEAD_DIM, :] * (1.0 / denom)).astype(_BF16)
        else:
            acc_ref[...] = acc

    pending = []

    def issue(n, consumed):
        if n < len(tiles):
            i = tiles[n][0]
            if tiles[n][1] == 0:
                assert all(tiles[m][0] % STATE_BUFFERS != i % STATE_BUFFERS
                           for m in range(consumed, n))
                start_query_tile(i)
            pending.append(scores(n))

    for n in range(SCORE_LEAD):
        issue(n, 0)
    for n in range(len(tiles)):
        issue(n + SCORE_LEAD, n)
        consume(n, pending.pop(0))


def _attention(qt, k, vt, c):
    B, H, _, S = qt.shape
    t = ATTN_TILE
    head_rows = pl.BlockSpec((1, 1, FOX_HEAD_DIM, S), lambda b, h: (b, h, 0, 0))
    return pl.pallas_call(
        functools.partial(_attn_kernel, t=t),
        grid=(B, H),
        in_specs=[
            head_rows,
            pl.BlockSpec((1, 1, S, PAIR_WIDTH), lambda b, h: (b, h, 0, 0)),
            head_rows,
            pl.BlockSpec((1, H, S), lambda b, h: (b, 0, 0)),
        ],
        out_specs=head_rows,
        out_shape=jax.ShapeDtypeStruct((B, H, FOX_HEAD_DIM, S), _BF16),
        scratch_shapes=(
            [pltpu.VMEM((PAIR_WIDTH, t), _BF16)] * STATE_BUFFERS
            + [pltpu.VMEM((1, t), _F32)] * STATE_BUFFERS
            + [pltpu.VMEM((FOX_HEAD_DIM + DENOM_ROWS, t), _F32)] * STATE_BUFFERS
        ),
        compiler_params=_params(("parallel", "parallel")),
        name="fox_attention",
    )(qt, k, vt, c)


def _memkv_kernel(m_ref, g_ref, wk_ref, wv_ref, k_ref, v_ref):
    m = _rms(m_ref[0], g_ref[...]).astype(_BF16)
    k_ref[0] = _dot(m, wk_ref[...]).astype(_BF16)
    v_ref[0] = _dot(m, wv_ref[...]).astype(_BF16)


def _memkv(mem, g, wk, wv):
    B, M, D = mem.shape
    spec = pl.BlockSpec((1, M, D), lambda b: (b, 0, 0))
    shape = jax.ShapeDtypeStruct((B, M, D), _BF16)
    return pl.pallas_call(
        _memkv_kernel,
        grid=(B,),
        in_specs=[spec, _const_spec(g.shape), _const_spec(wk.shape), _const_spec(wv.shape)],
        out_specs=[spec, spec],
        out_shape=[shape, shape],
        compiler_params=_params(("parallel",)),
        name="mem_kv",
    )(mem, g, wk, wv)


def _pool_mix(u, halo, first_row, pw_ref, ps_ref):
    tm = u.shape[0]
    pos = first_row + lax.broadcasted_iota(jnp.int32, (tm, 1), 0)
    outs = []
    for g, w in enumerate(POOL_WINDOWS):
        sl = slice(g * POOL_GROUP_DIM, (g + 1) * POOL_GROUP_DIM)
        ug = u[:, sl]
        ext = jnp.concatenate([halo[:, sl], ug], axis=0)
        d = 1
        while d < w:
            ext = ext[d:] + ext[:-d]
            d *= 2
        trailing = ext[ext.shape[0] - tm:]
        count = jnp.minimum(pos + 1, w).astype(_F32)
        pooled = (trailing / count - ug).astype(_BF16)
        outs.append(_dot(pooled, pw_ref[g]))
    return jnp.concatenate(outs, axis=1) * ps_ref[...]


def _memory_attention(h, wq_ref, k_ref, v_ref, wo_ref):
    q = (_dot(h, wq_ref[...]) * (1.0 / math.sqrt(X_HEAD_DIM))).astype(_BF16)
    heads = []
    for hd in range(X_HEADS):
        sl = slice(hd * X_HEAD_DIM, (hd + 1) * X_HEAD_DIM)
        s = _dot_nt(q[:, sl], k_ref[0, :, sl])
        e = jnp.exp(s - jnp.max(s, axis=-1, keepdims=True))
        p = e / jnp.sum(e, axis=-1, keepdims=True)
        heads.append(_dot(p.astype(_BF16), v_ref[0, :, sl]).astype(_BF16))
    return _dot(jnp.concatenate(heads, axis=1), wo_ref[...])


def _mix_xattn_kernel(x_ref, attnt_ref, u_ref, halo_ref, pw_ref, ps_ref, wo_ref, gmix_ref,
                      gpre_ref, wq_ref, k_ref, v_ref, wox_ref, gpost_ref, o_ref):
    i = pl.program_id(1)
    tm = u_ref.shape[1]
    sub = TOKEN_TILE
    chains = range(tm // sub)
    rows = [slice(r * sub, (r + 1) * sub) for r in chains]
    halos = [halo_ref[0] * (i > 0).astype(_F32) if r == 0
             else u_ref[0, r * sub - POOL_HALO:r * sub, :] for r in chains]
    pools = [_pool_mix(u_ref[0, rows[r], :], halos[r], i * tm + r * sub, pw_ref, ps_ref)
             for r in chains]
    mixes = [_dot_tn(attnt_ref[0, :, rows[r]], wo_ref[:FOX_WIDTH, :])
             + _dot(pools[r].astype(_BF16), wo_ref[FOX_WIDTH:, :]) for r in chains]
    xs = [x_ref[0, rows[r], :] + _rms(mixes[r], gmix_ref[...]) for r in chains]
    hs = [_rms(xs[r], gpre_ref[...]).astype(_BF16) for r in chains]
    for r in chains:
        o_ref[0, rows[r], :] = xs[r] + _rms(
            _memory_attention(hs[r], wq_ref, k_ref, v_ref, wox_ref), gpost_ref[...])


def _mix_xattn(x, attnt, u, pw, ps, wo, gmix, gpre, wq, mk, mv, wox, gpost):
    B, S, D = x.shape
    tm = MIX_CHAINS * TOKEN_TILE
    halo_blocks = tm // POOL_HALO
    xspec = pl.BlockSpec((1, tm, D), lambda b, i: (b, i, 0))
    kvspec = pl.BlockSpec((1, MEM_LEN, D), lambda b, i: (b, 0, 0))
    consts = (pw, ps, wo, gmix, gpre, wq)
    return pl.pallas_call(
        _mix_xattn_kernel,
        grid=(B, S // tm),
        in_specs=[
            xspec,
            pl.BlockSpec((1, FOX_WIDTH, tm), lambda b, i: (b, 0, i)),
            pl.BlockSpec((1, tm, POOL_WIDTH), lambda b, i: (b, i, 0)),
            pl.BlockSpec((1, POOL_HALO, POOL_WIDTH),
                         lambda b, i: (b, jnp.maximum(i * halo_blocks - 1, 0), 0)),
            *[_const_spec(a.shape) for a in consts],
            kvspec, kvspec, _const_spec(wox.shape), _const_spec(gpost.shape),
        ],
        out_specs=xspec,
        out_shape=jax.ShapeDtypeStruct(x.shape, _F32),
        compiler_params=_params(("parallel", "parallel")),
        name="mix_xattn",
    )(x, attnt, u, u, *consts, mk, mv, wox, gpost)


def _mlp_kernel(x_ref, gpre_ref, wup_ref, wdown_ref, gpost_ref, o_ref):
    x = x_ref[...]
    h = _rms(x, gpre_ref[...]).astype(_BF16)
    y = jnp.zeros(x.shape, _F32)
    for c in range(D_FF // FF_CHUNK):
        sl = slice(c * FF_CHUNK, (c + 1) * FF_CHUNK)
        a = jnp.square(jnp.maximum(_dot(h, wup_ref[:, sl]), 0.0))
        y = y + _dot(a.astype(_BF16), wdown_ref[sl, :])
    o_ref[...] = x + _rms(y, gpost_ref[...])


def _mlp(x2d, gpre, wup, wdown, gpost):
    T, D = x2d.shape
    tm = TOKEN_TILE
    xspec = pl.BlockSpec((tm, D), lambda i: (i, 0))
    return pl.pallas_call(
        _mlp_kernel,
        grid=(T // tm,),
        in_specs=[xspec, _const_spec(gpre.shape), _const_spec(wup.shape),
                  _const_spec(wdown.shape), _const_spec(gpost.shape)],
        out_specs=xspec,
        out_shape=jax.ShapeDtypeStruct(x2d.shape, _F32),
        compiler_params=_params(("parallel",)),
        name="relu2_mlp",
    )(x2d, gpre, wup, wdown, gpost)


def kernel(x, mem, g_mix_pre, w_in, b_forget, pool_w, pool_scale, w_out, g_mix_post,
           g_x_pre, g_mem, wq_x, wkv_x, wo_x, g_x_post, g_ffn_pre, w_up, w_down, g_ffn_post):
    B, S, D = x.shape
    depth = w_in.shape[0]
    o_f = 3 * FOX_WIDTH
    o_p = o_f + FOX_HEADS
    row = lambda a: a.reshape(1, -1)
    for l in range(depth):
        w = w_in[l].astype(_BF16)
        wq, wk, wv = (w[:, j * FOX_WIDTH:(j + 1) * FOX_WIDTH] for j in range(3))
        wf = jnp.pad(w[:, o_f:o_p].T, ((0, 16 - FOX_HEADS), (0, 0)))
        wu = w[:, o_p:]
        qt, k, vt, logf, u = _inproj(x, row(g_mix_pre[l]), wq.T, wk, wv.T, wf,
                                     b_forget[l].reshape(FOX_HEADS, 1), wu)
        c = _cumsum(logf.reshape(B * FOX_HEADS, S)).reshape(B, FOX_HEADS, S)
        kaug = _keyaug(k, jnp.swapaxes(c, 1, 2))
        per_head = (B, FOX_HEADS, FOX_HEAD_DIM, S)
        attnt = _attention(qt.reshape(per_head), kaug, vt.reshape(per_head), c)
        wkv = wkv_x[l].astype(_BF16)
        mk, mv = _memkv(mem, row(g_mem[l]), wkv[:, :D], wkv[:, D:])
        x = _mix_xattn(x, attnt.reshape(B, FOX_WIDTH, S), u, pool_w[l].astype(_BF16),
                       row(pool_scale[l]), w_out[l].astype(_BF16), row(g_mix_post[l]),
                       row(g_x_pre[l]), wq_x[l].astype(_BF16), mk, mv,
                       wo_x[l].astype(_BF16), row(g_x_post[l]))
        x = _mlp(x.reshape(B * S, D), row(g_ffn_pre[l]), w_up[l].astype(_BF16),
                 w_down[l].astype(_BF16), row(g_ffn_post[l])).reshape(B, S, D)
    return x
```

```python
import functools
import math

import jax
import jax.numpy as jnp
import numpy as np
from jax import lax
from jax.experimental import pallas as pl
from jax.experimental.pallas import tpu as pltpu

D_MODEL = 1024
FOX_WIDTH = 512
FOX_HEAD_DIM = 64
FOX_HEADS = 8
HEAD_PAIRS = FOX_HEADS // 2
PAIR_WIDTH = 2 * FOX_HEAD_DIM
DENOM_ROWS = 16
BIAS_PIECES = 3
POOL_WIDTH = 512
POOL_WINDOWS = (2, 4, 8, 16)
POOL_GROUP_DIM = 128
POOL_HALO = 16
MEM_LEN = 256
X_HEADS = 4
X_HEAD_DIM = 256
D_FF = 4096
EPS = 1e-6

LANES = 128
TOKEN_TILE = 512
ATTN_TILE = 512
MIX_CHAINS = 2
FF_CHUNK = 1024
VMEM_LIMIT = 48 * 1024 * 1024
MASK_VALUE = -1e30
LOG2E = math.log2(math.e)

_BF16 = jnp.bfloat16
_F32 = jnp.float32


def _rms(x, g):
    y = x * lax.rsqrt(jnp.mean(x * x, axis=-1, keepdims=True) + EPS)
    return y * g


def _dot(a, b):
    return jnp.dot(a, b, preferred_element_type=_F32)


def _dot_nt(a, b):
    return lax.dot_general(a, b, (((1,), (1,)), ((), ())), preferred_element_type=_F32)


def _dot_tn(a, b):
    return lax.dot_general(a, b, (((0,), (0,)), ((), ())), preferred_element_type=_F32)


def _split3(x):
    hi = x.astype(_BF16).astype(_F32)
    r1 = x - hi
    mid = r1.astype(_BF16).astype(_F32)
    lo = (r1 - mid).astype(_BF16).astype(_F32)
    return hi, mid, lo


def _const_spec(shape):
    nd = len(shape)
    return pl.BlockSpec(shape, lambda *_: (0,) * nd)


def _params(semantics):
    return pltpu.CompilerParams(dimension_semantics=semantics, vmem_limit_bytes=VMEM_LIMIT)


def _inproj_kernel(x_ref, g_ref, wqt_ref, wk_ref, wvt_ref, wf_ref, bf_ref, wu_ref,
                   qt_ref, k_ref, vt_ref, logf_ref, u_ref):
    h = _rms(x_ref[0], g_ref[...]).astype(_BF16)
    qt_ref[0] = (_dot_nt(wqt_ref[...], h) * (LOG2E / math.sqrt(FOX_HEAD_DIM))).astype(_BF16)
    vt_ref[0] = _dot_nt(wvt_ref[...], h).astype(_BF16)
    k = _dot(h, wk_ref[...])
    for p in range(HEAD_PAIRS):
        k_ref[0, p] = k[:, p * PAIR_WIDTH:(p + 1) * PAIR_WIDTH].astype(_BF16)
    u_ref[0] = _dot(h, wu_ref[...])
    z = _dot_nt(wf_ref[...], h)[:FOX_HEADS] + bf_ref[...]
    logf_ref[0] = jnp.minimum(z, 0.0) - jnp.log(1.0 + jnp.exp(-jnp.abs(z)))


def _inproj(x, g, wqt, wk, wvt, wf, bf, wu):
    B, S, D = x.shape
    tm = TOKEN_TILE
    t_shape = jax.ShapeDtypeStruct((B, FOX_WIDTH, S), _BF16)
    t_spec = pl.BlockSpec((1, FOX_WIDTH, tm), lambda b, i: (b, 0, i))
    return pl.pallas_call(
        _inproj_kernel,
        grid=(B, S // tm),
        in_specs=[
            pl.BlockSpec((1, tm, D), lambda b, i: (b, i, 0)),
            _const_spec(g.shape), _const_spec(wqt.shape), _const_spec(wk.shape),
            _const_spec(wvt.shape), _const_spec(wf.shape), _const_spec(bf.shape),
            _const_spec(wu.shape),
        ],
        out_specs=[
            t_spec,
            pl.BlockSpec((1, HEAD_PAIRS, tm, PAIR_WIDTH), lambda b, i: (b, 0, i, 0)),
            t_spec,
            pl.BlockSpec((1, FOX_HEADS, tm), lambda b, i: (b, 0, i)),
            pl.BlockSpec((1, tm, POOL_WIDTH), lambda b, i: (b, i, 0)),
        ],
        out_shape=[
            t_shape,
            jax.ShapeDtypeStruct((B, HEAD_PAIRS, S, PAIR_WIDTH), _BF16),
            t_shape,
            jax.ShapeDtypeStruct((B, FOX_HEADS, S), _F32),
            jax.ShapeDtypeStruct((B, S, POOL_WIDTH), _F32),
        ],
        compiler_params=_params(("parallel", "parallel")),
        name="inproj",
    )(x, g, wqt, wk, wvt, wf, bf, wu)


def _cumsum_kernel(x_ref, o_ref):
    rows, S = x_ref.shape
    r = lax.broadcasted_iota(jnp.int32, (LANES, LANES), 0)
    c = lax.broadcasted_iota(jnp.int32, (LANES, LANES), 1)
    tri = jnp.where(r <= c, 1.0, 0.0).astype(_BF16)
    carry = jnp.zeros((rows, 1), _F32)
    for j in range(S // LANES):
        hi, mid, lo = _split3(x_ref[:, j * LANES:(j + 1) * LANES])
        cs = (_dot(hi.astype(_BF16), tri) + _dot(mid.astype(_BF16), tri)
              + _dot(lo.astype(_BF16), tri))
        o_ref[:, j * LANES:(j + 1) * LANES] = (cs + carry) * LOG2E
        carry = carry + cs[:, LANES - 1:LANES]


def _cumsum(x):
    return pl.pallas_call(
        _cumsum_kernel,
        out_shape=jax.ShapeDtypeStruct(x.shape, _F32),
        compiler_params=pltpu.CompilerParams(vmem_limit_bytes=VMEM_LIMIT),
        name="gate_cumsum",
    )(x)


def _bias_lane_base(head):
    return FOX_HEAD_DIM if head % 2 == 0 else 0


PIECE_LANES = (BIAS_PIECES + 1) * FOX_HEADS


def _bias_placement():
    e = np.zeros((PIECE_LANES, FOX_HEADS * PAIR_WIDTH), np.float32)
    for h in range(FOX_HEADS):
        base = h * PAIR_WIDTH + _bias_lane_base(h)
        e[BIAS_PIECES * FOX_HEADS, base:base + BIAS_PIECES] = 1.0
        for p in range(BIAS_PIECES):
            e[p * FOX_HEADS + h, base + BIAS_PIECES + p] = -1.0
    return jnp.asarray(e, _BF16)


def _keyaug_kernel(k_ref, ct_ref, e_ref, o_ref):
    ts = k_ref.shape[2]
    x = ct_ref[0]
    hi, mid, lo = _split3(x)
    group = lax.broadcasted_iota(jnp.int32, x.shape, 1) // FOX_HEADS
    pieces = jnp.where(group == 0, hi, jnp.where(group == 1, mid, jnp.where(group == 2, lo, x)))
    aug = _dot(pieces.astype(_BF16), e_ref[...])
    lane = lax.broadcasted_iota(jnp.int32, (ts, PAIR_WIDTH), 1)
    for h in range(FOX_HEADS):
        own = (lane < FOX_HEAD_DIM) == (h % 2 == 0)
        slab = aug[:, h * PAIR_WIDTH:(h + 1) * PAIR_WIDTH].astype(_BF16)
        o_ref[0, h] = jnp.where(own, k_ref[0, h // 2], slab)


def _keyaug(k, ct):
    B, _, S, _ = k.shape
    ts = TOKEN_TILE
    e = _bias_placement()
    return pl.pallas_call(
        _keyaug_kernel,
        grid=(B, S // ts),
        in_specs=[
            pl.BlockSpec((1, HEAD_PAIRS, ts, PAIR_WIDTH), lambda b, i: (b, 0, i, 0)),
            pl.BlockSpec((1, ts, PIECE_LANES), lambda b, i: (b, i, 0)),
            _const_spec(e.shape),
        ],
        out_specs=pl.BlockSpec((1, FOX_HEADS, ts, PAIR_WIDTH), lambda b, i: (b, 0, i, 0)),
        out_shape=jax.ShapeDtypeStruct((B, FOX_HEADS, S, PAIR_WIDTH), _BF16),
        compiler_params=_params(("parallel", "parallel")),
        name="key_aug",
    )(k, ct, e)


SCORE_LEAD = 2
STATE_BUFFERS = 2


def _attn_kernel(qt_ref, k_ref, vt_ref, c_ref, o_ref, *scratch, t):
    q_scr = scratch[:STATE_BUFFERS]
    m_scr = scratch[STATE_BUFFERS:2 * STATE_BUFFERS]
    acc_scr = scratch[2 * STATE_BUFFERS:]
    S = k_ref.shape[2]
    h = pl.program_id(1)
    tiles = [(i, j) for i in range(S // t) for j in range(i + 1)]

    def start_query_tile(i):
        slot = i % STATE_BUFFERS
        cols = slice(i * t, (i + 1) * t)
        hi, mid, lo = _split3(c_ref[0, pl.ds(h, 1), cols])
        row = lax.broadcasted_iota(jnp.int32, (FOX_HEAD_DIM, t), 0)
        aug = jnp.where(row < 2 * BIAS_PIECES, 1.0, 0.0)
        for p, piece in enumerate((hi, mid, lo)):
            aug = jnp.where(row == p, piece, aug)
        aug = aug.astype(_BF16)
        qt = qt_ref[0, 0, :, cols]
        even = jnp.concatenate([qt, aug], axis=0)
        odd = jnp.concatenate([aug, qt], axis=0)
        q_scr[slot][...] = jnp.where(h % 2 == 0, even, odd)
        m_scr[slot][...] = jnp.full((1, t), MASK_VALUE, _F32)
        acc_scr[slot][...] = jnp.zeros(acc_scr[slot].shape, _F32)

    def scores(n):
        i, j = tiles[n]
        return _dot(k_ref[0, 0, j * t:(j + 1) * t, :],
                    q_scr[i % STATE_BUFFERS][...])

    def consume(n, st):
        i, j = tiles[n]
        m_ref, acc_ref = m_scr[i % STATE_BUFFERS], acc_scr[i % STATE_BUFFERS]
        vt = jnp.concatenate([vt_ref[0, 0, :, j * t:(j + 1) * t],
                              jnp.ones((DENOM_ROWS, t), _BF16)], axis=0)
        if j == i:
            key = lax.broadcasted_iota(jnp.int32, (t, t), 0)
            qry = lax.broadcasted_iota(jnp.int32, (t, t), 1)
            st = jnp.where(key <= qry, st, MASK_VALUE)
        m_old = m_ref[...]
        m_new = jnp.maximum(m_old, jnp.max(st, axis=0, keepdims=True))
        alpha = jnp.exp2(m_old - m_new)
        p = jnp.exp2(st - m_new).astype(_BF16)
        m_ref[...] = m_new
        acc = acc_ref[...] * alpha + _dot(vt, p)
        if j == i:
            denom = acc[FOX_HEAD_DIM:FOX_HEAD_DIM + 1, :]
            o_ref[0, 0, :, i * t:(i + 1) * t] = (acc[:FOX_HEAD_DIM, :] * (1.0 / denom)).astype(_BF16)
        else:
            acc_ref[...] = acc

    pending = []

    def issue(n, consumed):
        if n < len(tiles):
            i = tiles[n][0]
            if tiles[n][1] == 0:
                assert all(tiles[m][0] % STATE_BUFFERS != i % STATE_BUFFERS
                           for m in range(consumed, n))
                start_query_tile(i)
            pending.append(scores(n))

    for n in range(SCORE_LEAD):
        issue(n, 0)
    for n in range(len(tiles)):
        issue(n + SCORE_LEAD, n)
        consume(n, pending.pop(0))


def _attention(qt, k, vt, c):
    B, H, _, S = qt.shape
    t = ATTN_TILE
    head_rows = pl.BlockSpec((1, 1, FOX_HEAD_DIM, S), lambda b, h: (b, h, 0, 0))
    return pl.pallas_call(
        functools.partial(_attn_kernel, t=t),
        grid=(B, H),
        in_specs=[
            head_rows,
            pl.BlockSpec((1, 1, S, PAIR_WIDTH), lambda b, h: (b, h, 0, 0)),
            head_rows,
            pl.BlockSpec((1, H, S), lambda b, h: (b, 0, 0)),
        ],
        out_specs=head_rows,
        out_shape=jax.ShapeDtypeStruct((B, H, FOX_HEAD_DIM, S), _BF16),
        scratch_shapes=(
            [pltpu.VMEM((PAIR_WIDTH, t), _BF16)] * STATE_BUFFERS
            + [pltpu.VMEM((1, t), _F32)] * STATE_BUFFERS
            + [pltpu.VMEM((FOX_HEAD_DIM + DENOM_ROWS, t), _F32)] * STATE_BUFFERS
        ),
        compiler_params=_params(("parallel", "parallel")),
        name="fox_attention",
    )(qt, k, vt, c)


def _memkv_kernel(m_ref, g_ref, wk_ref, wv_ref, k_ref, v_ref):
    m = _rms(m_ref[0], g_ref[...]).astype(_BF16)
    k_ref[0] = _dot(m, wk_ref[...]).astype(_BF16)
    v_ref[0] = _dot(m, wv_ref[...]).astype(_BF16)


def _memkv(mem, g, wk, wv):
    B, M, D = mem.shape
    spec = pl.BlockSpec((1, M, D), lambda b: (b, 0, 0))
    shape = jax.ShapeDtypeStruct((B, M, D), _BF16)
    return pl.pallas_call(
        _memkv_kernel,
        grid=(B,),
        in_specs=[spec, _const_spec(g.shape), _const_spec(wk.shape), _const_spec(wv.shape)],
        out_specs=[spec, spec],
        out_shape=[shape, shape],
        compiler_params=_params(("parallel",)),
        name="mem_kv",
    )(mem, g, wk, wv)


def _pool_mix(u, halo, first_row, pw_ref, ps_ref):
    tm = u.shape[0]
    pos = first_row + lax.broadcasted_iota(jnp.int32, (tm, 1), 0)
    outs = []
    for g, w in enumerate(POOL_WINDOWS):
        sl = slice(g * POOL_GROUP_DIM, (g + 1) * POOL_GROUP_DIM)
        ug = u[:, sl]
        ext = jnp.concatenate([halo[:, sl], ug], axis=0)
        d = 1
        while d < w:
            ext = ext[d:] + ext[:-d]
            d *= 2
        trailing = ext[ext.shape[0] - tm:]
        count = jnp.minimum(pos + 1, w).astype(_F32)
        pooled = (trailing / count - ug).astype(_BF16)
        outs.append(_dot(pooled, pw_ref[g]))
    return jnp.concatenate(outs, axis=1) * ps_ref[...]


def _memory_attention(h, wq_ref, k_ref, v_ref, wo_ref):
    q = (_dot(h, wq_ref[...]) * (1.0 / math.sqrt(X_HEAD_DIM))).astype(_BF16)
    heads = []
    for hd in range(X_HEADS):
        sl = slice(hd * X_HEAD_DIM, (hd + 1) * X_HEAD_DIM)
        s = _dot_nt(q[:, sl], k_ref[0, :, sl])
        e = jnp.exp(s - jnp.max(s, axis=-1, keepdims=True))
        p = e / jnp.sum(e, axis=-1, keepdims=True)
        heads.append(_dot(p.astype(_BF16), v_ref[0, :, sl]).astype(_BF16))
    return _dot(jnp.concatenate(heads, axis=1), wo_ref[...])


def _mix_xattn_kernel(x_ref, attnt_ref, u_ref, halo_ref, pw_ref, ps_ref, wo_ref, gmix_ref,
                      gpre_ref, wq_ref, k_ref, v_ref, wox_ref, gpost_ref, o_ref):
    i = pl.program_id(1)
    tm = u_ref.shape[1]
    sub = TOKEN_TILE
    chains = range(tm // sub)
    rows = [slice(r * sub, (r + 1) * sub) for r in chains]
    halos = [halo_ref[0] * (i > 0).astype(_F32) if r == 0
             else u_ref[0, r * sub - POOL_HALO:r * sub, :] for r in chains]
    pools = [_pool_mix(u_ref[0, rows[r], :], halos[r], i * tm + r * sub, pw_ref, ps_ref)
             for r in chains]
    mixes = [_dot_tn(attnt_ref[0, :, rows[r]], wo_ref[:FOX_WIDTH, :])
             + _dot(pools[r].astype(_BF16), wo_ref[FOX_WIDTH:, :]) for r in chains]
    xs = [x_ref[0, rows[r], :] + _rms(mixes[r], gmix_ref[...]) for r in chains]
    hs = [_rms(xs[r], gpre_ref[...]).astype(_BF16) for r in chains]
    for r in chains:
        o_ref[0, rows[r], :] = xs[r] + _rms(
            _memory_attention(hs[r], wq_ref, k_ref, v_ref, wox_ref), gpost_ref[...])


def _mix_xattn(x, attnt, u, pw, ps, wo, gmix, gpre, wq, mk, mv, wox, gpost):
    B, S, D = x.shape
    tm = MIX_CHAINS * TOKEN_TILE
    halo_blocks = tm // POOL_HALO
    xspec = pl.BlockSpec((1, tm, D), lambda b, i: (b, i, 0))
    kvspec = pl.BlockSpec((1, MEM_LEN, D), lambda b, i: (b, 0, 0))
    consts = (pw, ps, wo, gmix, gpre, wq)
    return pl.pallas_call(
        _mix_xattn_kernel,
        grid=(B, S // tm),
        in_specs=[
            xspec,
            pl.BlockSpec((1, FOX_WIDTH, tm), lambda b, i: (b, 0, i)),
            pl.BlockSpec((1, tm, POOL_WIDTH), lambda b, i: (b, i, 0)),
            pl.BlockSpec((1, POOL_HALO, POOL_WIDTH),
                         lambda b, i: (b, jnp.maximum(i * halo_blocks - 1, 0), 0)),
            *[_const_spec(a.shape) for a in consts],
            kvspec, kvspec, _const_spec(wox.shape), _const_spec(gpost.shape),
        ],
        out_specs=xspec,
        out_shape=jax.ShapeDtypeStruct(x.shape, _F32),
        compiler_params=_params(("parallel", "parallel")),
        name="mix_xattn",
    )(x, attnt, u, u, *consts, mk, mv, wox, gpost)


def _mlp_kernel(x_ref, gpre_ref, wup_ref, wdown_ref, gpost_ref, o_ref):
    x = x_ref[...]
    h = _rms(x, gpre_ref[...]).astype(_BF16)
    y = jnp.zeros(x.shape, _F32)
    for c in range(D_FF // FF_CHUNK):
        sl = slice(c * FF_CHUNK, (c + 1) * FF_CHUNK)
        a = jnp.square(jnp.maximum(_dot(h, wup_ref[:, sl]), 0.0))
        y = y + _dot(a.astype(_BF16), wdown_ref[sl, :])
    o_ref[...] = x + _rms(y, gpost_ref[...])


def _mlp(x2d, gpre, wup, wdown, gpost):
    T, D = x2d.shape
    tm = TOKEN_TILE
    xspec = pl.BlockSpec((tm, D), lambda i: (i, 0))
    return pl.pallas_call(
        _mlp_kernel,
        grid=(T // tm,),
        in_specs=[xspec, _const_spec(gpre.shape), _const_spec(wup.shape),
                  _const_spec(wdown.shape), _const_spec(gpost.shape)],
        out_specs=xspec,
        out_shape=jax.ShapeDtypeStruct(x2d.shape, _F32),
        compiler_params=_params(("parallel",)),
        name="relu2_mlp",
    )(x2d, gpre, wup, wdown, gpost)


def kernel(x, mem, g_mix_pre, w_in, b_forget, pool_w, pool_scale, w_out, g_mix_post,
           g_x_pre, g_mem, wq_x, wkv_x, wo_x, g_x_post, g_ffn_pre, w_up, w_down, g_ffn_post):
    B, S, D = x.shape
    depth = w_in.shape[0]
    o_f = 3 * FOX_WIDTH
    o_p = o_f + FOX_HEADS
    row = lambda a: a.reshape(1, -1)
    for l in range(depth):
        w = w_in[l].astype(_BF16)
        wq, wk, wv = (w[:, j * FOX_WIDTH:(j + 1) * FOX_WIDTH] for j in range(3))
        wf = jnp.pad(w[:, o_f:o_p].T, ((0, 16 - FOX_HEADS), (0, 0)))
        wu = w[:, o_p:]
        qt, k, vt, logf, u = _inproj(x, row(g_mix_pre[l]), wq.T, wk, wv.T, wf,
                                     b_forget[l].reshape(FOX_HEADS, 1), wu)
        c = _cumsum(logf.reshape(B * FOX_HEADS, S)).reshape(B, FOX_HEADS, S)
        ct = jnp.swapaxes(c, 1, 2)
        kaug = _keyaug(k, jnp.concatenate([ct] * BIAS_PIECES + [jnp.ones_like(ct)], axis=-1))
        per_head = (B, FOX_HEADS, FOX_HEAD_DIM, S)
        attnt = _attention(qt.reshape(per_head), kaug, vt.reshape(per_head), c)
        wkv = wkv_x[l].astype(_BF16)
        mk, mv = _memkv(mem, row(g_mem[l]), wkv[:, :D], wkv[:, D:])
        x = _mix_xattn(x, attnt.reshape(B, FOX_WIDTH, S), u, pool_w[l].astype(_BF16),
                       row(pool_scale[l]), w_out[l].astype(_BF16), row(g_mix_post[l]),
                       row(g_x_pre[l]), wq_x[l].astype(_BF16), mk, mv,
                       wo_x[l].astype(_BF16), row(g_x_post[l]))
        x = _mlp(x.reshape(B * S, D), row(g_ffn_pre[l]), w_up[l].astype(_BF16),
                 w_down[l].astype(_BF16), row(g_ffn_post[l])).reshape(B, S, D)
    return x
```
